```python
import math
import jax, jax.numpy as jnp
from jax import lax
import numpy as np

D_MODEL = 1024
BATCH = 8
SEQ = 2048
DEPTH = 4
DEC_BATCH = 32
DEC_SEQ = 1
PAST_LEN = 16384
PAGE_SIZE = 128

N_META = 16
EPS = 1e-6
BLOCK = 128
ROPE_THETA = 10000.0
N_EVEN = (DEPTH + 1) // 2
N_ODD = DEPTH // 2

A_HEADS = 4
A_KDIM = 128
A_VDIM = 128
A_WIDTH = A_HEADS * A_VDIM
B_HEADS = 4
B_HDIM = 128
B_WIDTH = B_HEADS * B_HDIM
EVEN_SIZES = (A_HEADS * A_KDIM, A_HEADS * A_KDIM, A_WIDTH, A_WIDTH, B_WIDTH, B_WIDTH, B_WIDTH, B_WIDTH)
EVEN_IN = sum(EVEN_SIZES)
EVEN_OUT = A_WIDTH + B_WIDTH

C_HEADS = 8
C_NOPE = 128
C_ROPE = 64
C_VDIM = 128
C_WIDTH = C_HEADS * C_VDIM
Q_LORA = 384
KV_LORA = 256
ODD_SIZES = (Q_LORA, KV_LORA, C_ROPE, C_WIDTH)
ODD_IN = sum(ODD_SIZES)

kernel_name = 'hybrid_hgrn2_stickbreak_mla_step'


def rmsnorm(x, gain):
    xf = x.astype(jnp.float32)
    y = xf * lax.rsqrt(jnp.mean(xf * xf, axis=-1, keepdims=True) + EPS)
    return (y * gain.astype(jnp.float32)).astype(x.dtype)


def rope(x, pos):
    half = x.shape[-1] // 2
    inv = ROPE_THETA ** (-jnp.arange(half, dtype=jnp.float32) / half)
    ang = pos.astype(jnp.float32)[:, None] * inv[None, :]
    cos = jnp.cos(ang)[None, :, None, :]
    sin = jnp.sin(ang)[None, :, None, :]
    xf = x.astype(jnp.float32)
    x1, x2 = xf[..., :half], xf[..., half:]
    return jnp.concatenate([x1 * cos - x2 * sin, x2 * cos + x1 * sin], axis=-1).astype(x.dtype)


def split_cols(a, sizes):
    offs = np.cumsum(np.array(sizes))[:-1].tolist()
    return jnp.split(a, offs, axis=-1)


def to_chunks(a):
    b, t = a.shape[0], a.shape[1]
    return jnp.moveaxis(a.reshape((b, t // BLOCK, BLOCK) + a.shape[2:]), 1, 0)


def from_chunks(a):
    n, b, c = a.shape[0], a.shape[1], a.shape[2]
    return jnp.moveaxis(a, 0, 1).reshape((b, n * c) + a.shape[3:])


def gather_pages(pool, page_table):
    g = pool[page_table]
    return g.reshape((g.shape[0], g.shape[1] * g.shape[2]) + g.shape[3:])


def blocked_queries(attend, qs, q_pos):
    head = attend(tuple(a[:, :N_META] for a in qs), q_pos[:N_META])
    rest = tuple(to_chunks(a[:, N_META:]) for a in qs)
    pos_rest = q_pos[N_META:].reshape(-1, BLOCK)
    outs = lax.map(lambda c: attend(c[0], c[1]), (rest, pos_rest))
    return jnp.concatenate([head, from_chunks(outs)], axis=1)


def gla_chunk(S0, q, k, v, logf):
    f32 = jnp.float32
    q, k, v, logf = (a.astype(f32) for a in (q, k, v, logf))
    S0 = S0.astype(f32)
    C = q.shape[1]
    b = jnp.cumsum(logf, axis=1)
    o_inter = jnp.einsum('bthk,bhkv->bthv', q * jnp.exp(b), S0)
    causal = jnp.tril(jnp.ones((C, C), dtype=bool))[None, :, :, None, None]
    decay = jnp.exp(jnp.where(causal, b[:, :, None] - b[:, None, :], -jnp.inf))
    att = jnp.sum(q[:, :, None] * k[:, None, :] * decay, axis=-1)
    o = o_inter + jnp.einsum('btsh,bshv->bthv', att, v)
    b_last = b[:, -1]
    S = jnp.exp(b_last)[..., None] * S0 + jnp.einsum('bshk,bshv->bhkv', k * jnp.exp(b_last[:, None] - b), v)
    return o, S


def hgrn_prompt(q, k, v, logf, S0):
    o_head, S = gla_chunk(S0, q[:, :N_META], k[:, :N_META], v[:, :N_META], logf[:, :N_META])
    xs = tuple(to_chunks(a[:, N_META:]) for a in (q, k, v, logf))

    def body(S, c):
        o, S = gla_chunk(S, c[0], c[1], c[2], c[3])
        return S, o

    S, o_rest = lax.scan(body, S, xs)
    return jnp.concatenate([o_head, from_chunks(o_rest)], axis=1), S


def sb_attend(q, k, v, q_pos, k_pos):
    z = jnp.einsum('bqhd,bkhd->bhqk', q, k, preferred_element_type=jnp.float32) * (q.shape[-1] ** -0.5)
    visible = (k_pos[None, :] < q_pos[:, None])[None, None]
    sp = jnp.where(visible, jax.nn.softplus(z), 0.0)
    suffix = lax.cumsum(sp, axis=3, reverse=True) - sp
    A = jnp.where(visible, jnp.exp(jax.nn.log_sigmoid(z) - suffix), 0.0)
    return jnp.einsum('bhqk,bkhd->bqhd', A.astype(v.dtype), v)


def mla_attend(qn, qr, kn, kr, c, q_pos, k_pos):
    s = (jnp.einsum('bqhd,bkhd->bhqk', qn, kn, preferred_element_type=jnp.float32)
         + jnp.einsum('bqhd,bkd->bhqk', qr, kr, preferred_element_type=jnp.float32))
    s = s * ((C_NOPE + C_ROPE) ** -0.5)
    s = jnp.where((k_pos[None, :] <= q_pos[:, None])[None, None], s, -jnp.inf)
    p = jax.nn.softmax(s, axis=-1)
    return jnp.einsum('bhqk,bkc->bqhc', p.astype(c.dtype), c)


def even_mixer(h, pos, lb, w_in, w_out, out_gain, S0, k_past, v_past):
    Bn, T = h.shape[0], h.shape[1]
    qA, fA, iA, gA, qB, kB, vB, gB = split_cols(h @ w_in, EVEN_SIZES)
    qA = qA.reshape(Bn, T, A_HEADS, A_KDIM)
    iA = iA.reshape(Bn, T, A_HEADS, A_VDIM)
    lbh = lb.reshape(A_HEADS, A_KDIM)
    f = lbh + (1.0 - lbh) * jax.nn.sigmoid(fA.astype(jnp.float32).reshape(Bn, T, A_HEADS, A_KDIM))
    logf = jnp.log(f)
    kA = 1.0 - f
    if k_past is None:
        oA, S_new = hgrn_prompt(qA, kA, iA, logf, S0)
    else:
        oA, S_new = gla_chunk(S0, qA, kA, iA, logf)
    oA = rmsnorm(oA.astype(h.dtype), out_gain).reshape(Bn, T, A_WIDTH) * jax.nn.silu(gA)
    qB = qB.reshape(Bn, T, B_HEADS, B_HDIM)
    kB = kB.reshape(Bn, T, B_HEADS, B_HDIM)
    vB = vB.reshape(Bn, T, B_HEADS, B_HDIM)
    if k_past is None:
        k_all, v_all, k_pos = kB, vB, pos
        oB = blocked_queries(lambda qs, pb: sb_attend(qs[0], k_all, v_all, pb, k_pos), (qB,), pos)
    else:
        k_all = jnp.concatenate([k_past, kB], axis=1)
        v_all = jnp.concatenate([v_past, vB], axis=1)
        k_pos = jnp.arange(k_all.shape[1], dtype=jnp.int32)
        oB = sb_attend(qB, k_all, v_all, pos, k_pos)
    oB = oB.reshape(Bn, T, B_WIDTH) * jax.nn.silu(gB)
    out = jnp.concatenate([oA, oB], axis=-1) @ w_out
    return out, S_new.astype(S0.dtype), kB, vB


def odd_mixer(h, pos, w_in, q_a_gain, w_uq, kv_a_gain, w_uk, w_uv, g_qn, g_qr, g_kn, g_kr, w_out, c_past, kr_past):
    Bn, T = h.shape[0], h.shape[1]
    cq, ckv, kr, g = split_cols(h @ w_in, ODD_SIZES)
    q = (rmsnorm(cq, q_a_gain) @ w_uq).reshape(Bn, T, C_HEADS, C_NOPE + C_ROPE)
    qn = rmsnorm(q[..., :C_NOPE], g_qn)
    qr = rope(rmsnorm(q[..., C_NOPE:], g_qr), pos)
    c_new = rmsnorm(ckv, kv_a_gain)
    kr_new = rope(rmsnorm(kr, g_kr)[:, :, None, :], pos)[:, :, 0]
    if c_past is None:
        c_all, kr_all, k_pos = c_new, kr_new, pos
    else:
        c_all = jnp.concatenate([c_past, c_new], axis=1)
        kr_all = jnp.concatenate([kr_past, kr_new], axis=1)
        k_pos = jnp.arange(c_all.shape[1], dtype=jnp.int32)
    kn = rmsnorm(jnp.einsum('btc,chd->bthd', c_all, w_uk.reshape(KV_LORA, C_HEADS, C_NOPE)), g_kn)
    attend = lambda qs, pb: mla_attend(qs[0], qs[1], kn, kr_all, c_all, pb, k_pos)
    if c_past is None:
        ctx = blocked_queries(attend, (qn, qr), pos)
    else:
        ctx = attend((qn, qr), pos)
    o = jnp.einsum('bqhc,chd->bqhd', ctx, w_uv.reshape(KV_LORA, C_HEADS, C_VDIM)).reshape(Bn, T, C_WIDTH)
    o = o * jax.nn.silu(g)
    return o @ w_out, c_new, kr_new


def trunk(x, pos, weights, state, paged):
    (norm_gain, lower_bounds, w_in_even, hgrn_out_gain, w_out_even, w_in_odd, q_a_gain, w_uq,
     kv_a_gain, w_uk, w_uv, g_qn, g_qr, g_kn, g_kr, w_out_odd) = weights
    new_S, new_k, new_v, new_c, new_kr = [], [], [], [], []
    for layer in range(DEPTH):
        h = rmsnorm(x, norm_gain[layer])
        if layer % 2 == 0:
            e = layer // 2
            k_past = v_past = None
            if paged is not None:
                k_past = gather_pages(paged[0][e], paged[4])
                v_past = gather_pages(paged[1][e], paged[4])
            out, S, kb, vb = even_mixer(h, pos, lower_bounds[layer], w_in_even[e], w_out_even[e],
                                        hgrn_out_gain[e], state[e], k_past, v_past)
            new_S.append(S)
            new_k.append(kb)
            new_v.append(vb)
        else:
            o = layer // 2
            c_past = kr_past = None
            if paged is not None:
                c_past = gather_pages(paged[2][o], paged[4])
                kr_past = gather_pages(paged[3][o], paged[4])
            out, cn, krn = odd_mixer(h, pos, w_in_odd[o], q_a_gain[o], w_uq[o], kv_a_gain[o], w_uk[o], w_uv[o],
                                     g_qn[o], g_qr[o], g_kn[o], g_kr[o], w_out_odd[o], c_past, kr_past)
            new_c.append(cn)
            new_kr.append(krn)
        x = x + out
    return x, jnp.stack(new_S), jnp.stack(new_k), jnp.stack(new_v), jnp.stack(new_c), jnp.stack(new_kr)


def setup_inputs(seed: int = 0) -> dict:
    key = jax.random.key(seed)
    ks = jax.random.split(key, 26)
    f32 = jnp.float32
    n_pages = PAST_LEN // PAGE_SIZE
    n_phys = (DEC_BATCH * n_pages * 5) // 4

    def nrm(k, shape, scale=1.0):
        return jax.random.normal(k, shape, f32) * scale

    def gain(k, shape):
        return 1.0 + 0.02 * jax.random.normal(k, shape, f32)

    page_table = jax.random.permutation(ks[7], n_phys)[: DEC_BATCH * n_pages].reshape(DEC_BATCH, n_pages).astype(jnp.int32)
    return {
        'x_prompt': nrm(ks[0], (BATCH, SEQ, D_MODEL)),
        'x_sample': nrm(ks[1], (DEC_BATCH, DEC_SEQ, D_MODEL)),
        'state_hgrn': nrm(ks[2], (N_EVEN, DEC_BATCH, A_HEADS, A_KDIM, A_VDIM), 0.5),
        'cache_sb_k': nrm(ks[3], (N_EVEN, n_phys, PAGE_SIZE, B_HEADS, B_HDIM)),
        'cache_sb_v': nrm(ks[4], (N_EVEN, n_phys, PAGE_SIZE, B_HEADS, B_HDIM)),
        'cache_mla_ckv': nrm(ks[5], (N_ODD, n_phys, PAGE_SIZE, KV_LORA)),
        'cache_mla_krope': nrm(ks[6], (N_ODD, n_phys, PAGE_SIZE, C_ROPE)),
        'page_table': page_table,
        'meta_tokens': nrm(ks[8], (N_META, D_MODEL)),
        'norm_gain': gain(ks[9], (DEPTH, D_MODEL)),
        'lb_logits': nrm(ks[10], (DEPTH, A_HEADS * A_KDIM), 0.5),
        'w_in_even': nrm(ks[11], (N_EVEN, D_MODEL, EVEN_IN), D_MODEL ** -0.5),
        'hgrn_out_gain': gain(ks[12], (N_EVEN, A_VDIM)),
        'w_out_even': nrm(ks[13], (N_EVEN, EVEN_OUT, D_MODEL), EVEN_OUT ** -0.5),
        'w_in_odd': nrm(ks[14], (N_ODD, D_MODEL, ODD_IN), D_MODEL ** -0.5),
        'q_a_gain': gain(ks[15], (N_ODD, Q_LORA)),
        'w_uq': nrm(ks[16], (N_ODD, Q_LORA, C_HEADS * (C_NOPE + C_ROPE)), Q_LORA ** -0.5),
        'kv_a_gain': gain(ks[17], (N_ODD, KV_LORA)),
        'w_uk': nrm(ks[18], (N_ODD, KV_LORA, C_HEADS * C_NOPE), KV_LORA ** -0.5),
        'w_uv': nrm(ks[19], (N_ODD, KV_LORA, C_HEADS * C_VDIM), KV_LORA ** -0.5),
        'qk_gain_q_nope': gain(ks[20], (N_ODD, C_NOPE)),
        'qk_gain_q_rope': gain(ks[21], (N_ODD, C_ROPE)),
        'qk_gain_k_nope': gain(ks[22], (N_ODD, C_NOPE)),
        'qk_gain_k_rope': gain(ks[23], (N_ODD, C_ROPE)),
        'w_out_odd': nrm(ks[24], (N_ODD, C_WIDTH, D_MODEL), C_WIDTH ** -0.5),
    }


def reference(x_prompt, x_sample, state_hgrn, cache_sb_k, cache_sb_v, cache_mla_ckv, cache_mla_krope, page_table,
              meta_tokens, norm_gain, lb_logits, w_in_even, hgrn_out_gain, w_out_even, w_in_odd, q_a_gain, w_uq,
              kv_a_gain, w_uk, w_uv, qk_gain_q_nope, qk_gain_q_rope, qk_gain_k_nope, qk_gain_k_rope, w_out_odd):
    sm = jax.nn.softmax(lb_logits.astype(jnp.float32), axis=0)
    lower_bounds = jnp.cumsum(sm, axis=0) - sm[:1]
    weights = (norm_gain, lower_bounds, w_in_even, hgrn_out_gain, w_out_even, w_in_odd, q_a_gain, w_uq,
               kv_a_gain, w_uk, w_uv, qk_gain_q_nope, qk_gain_q_rope, qk_gain_k_nope, qk_gain_k_rope, w_out_odd)
    bp = x_prompt.shape[0]
    meta = jnp.broadcast_to(meta_tokens[None].astype(x_prompt.dtype), (bp, N_META, D_MODEL))
    xp = jnp.concatenate([meta, x_prompt], axis=1)
    pos_p = jnp.arange(xp.shape[1], dtype=jnp.int32)
    zero_state = jnp.zeros((N_EVEN, bp, A_HEADS, A_KDIM, A_VDIM), x_prompt.dtype)
    yp, p_S, p_k, p_v, p_c, p_kr = trunk(xp, pos_p, weights, zero_state, None)
    y_prompt = yp[:, N_META:]
    pos_s = PAST_LEN + jnp.arange(x_sample.shape[1], dtype=jnp.int32)
    y_sample, s_S, s_k, s_v, s_c, s_kr = trunk(x_sample, pos_s, weights, state_hgrn,
                                               (cache_sb_k, cache_sb_v, cache_mla_ckv, cache_mla_krope, page_table))
    return (y_prompt, y_sample, p_S, p_k, p_v, p_c, p_kr, s_S, s_k, s_v, s_c, s_kr)
```

```python
import functools

import numpy as np
import jax
import jax.numpy as jnp
from jax import lax
from jax.experimental import pallas as pl
from jax.experimental.pallas import tpu as pltpu

EPS = 1e-6
ROPE_THETA = 10000.0
LANES = 128
VMEM_LIMIT = 56 * 1024 * 1024
F32 = jnp.float32
BF16 = jnp.bfloat16
GLA_LEVELS = (64, 32, 16, 8, 4, 2, 1)

NT = (((1,), (1,)), ((), ()))
TN = (((0,), (0,)), ((), ()))


def _params(*sem):
    return pltpu.CompilerParams(dimension_semantics=sem, vmem_limit_bytes=VMEM_LIMIT)


def _dot(a, b):
    return jnp.dot(a, b, preferred_element_type=F32)


def _dot_nt(a, b):
    return lax.dot_general(a, b, NT, preferred_element_type=F32)


def _rms(x, gain, n=None):
    ss = jnp.sum(x * x, axis=-1, keepdims=True)
    ms = ss / (x.shape[-1] if n is None else n)
    return x * lax.rsqrt(ms + EPS) * gain


def _sigmoid(x):
    return 1.0 / (1.0 + jnp.exp(-x))


def _silu(x):
    return x * _sigmoid(x)


def _softplus(z):
    return jnp.maximum(z, 0.0) + jnp.log1p(jnp.exp(-jnp.abs(z)))


def _split2(x):
    hi = x.astype(BF16)
    lo = (x - hi.astype(F32)).astype(BF16)
    return hi, lo


def _split3(x):
    hi = x.astype(BF16)
    r = x - hi.astype(F32)
    mid = r.astype(BF16)
    lo = (r - mid.astype(F32)).astype(BF16)
    return hi, mid, lo


def _rope(x, cos_t, sin_t, half):
    lane = lax.broadcasted_iota(jnp.int32, x.shape, 1)
    swapped = jnp.where(lane < half, pltpu.roll(x, LANES - half, 1), pltpu.roll(x, half, 1))
    return x * cos_t + swapped * sin_t


def _even_in_kernel(layer, x_ref, ng_ref, lbl_ref, w_ref,
                    qa_ref, lf_ref, ka_ref, ia_ref, ga_ref, qb_ref, kb_ref, vb_ref, kb16_ref, vb16_ref, gb_ref):
    gw = qa_ref.shape[-1]
    hb = _rms(x_ref[...], ng_ref[...]).astype(BF16)

    def proj(g):
        return _dot(hb, w_ref[:, g * gw:(g + 1) * gw])

    logits = lbl_ref[...]
    e = jnp.exp(logits - jnp.max(logits, axis=0, keepdims=True))
    sm = e / jnp.sum(e, axis=0, keepdims=True)
    cum = sm[0:1]
    for i in range(1, layer + 1):
        cum = cum + sm[i:i + 1]
    lb = cum - sm[0:1]

    qa_ref[...] = proj(0)
    f = lb + (1.0 - lb) * _sigmoid(proj(1))
    lf_ref[...] = jnp.log(f)
    ka_ref[...] = 1.0 - f
    ia_ref[...] = proj(2)
    ga_ref[...] = proj(3)
    qb_ref[...] = proj(4).astype(BF16)
    kb = proj(5)
    kb_ref[...] = kb
    kb16_ref[...] = kb.astype(BF16)
    vb = proj(6)
    vb_ref[...] = vb
    vb16_ref[...] = vb.astype(BF16)
    gb_ref[...] = proj(7)


def _even_in(x, ng, lbl, w, layer, tm):
    bn, ln, d = x.shape
    gw = w.shape[1] // 8
    row = lambda b, j: (b, j, 0)
    const = lambda b, j: (0, 0)
    out_dt = [F32, F32, F32, F32, F32, BF16, F32, F32, BF16, BF16, F32]
    return pl.pallas_call(
        functools.partial(_even_in_kernel, layer),
        grid=(bn, ln // tm),
        in_specs=[pl.BlockSpec((None, tm, d), row), pl.BlockSpec((1, d), const),
                  pl.BlockSpec(lbl.shape, const), pl.BlockSpec(w.shape, const)],
        out_specs=[pl.BlockSpec((None, tm, gw), row)] * len(out_dt),
        out_shape=[jax.ShapeDtypeStruct((bn, ln, gw), dt) for dt in out_dt],
        compiler_params=_params("parallel", "parallel"),
        name="even_in",
    )(x, ng, lbl, w)


def _gla_constants():
    c = LANES
    t = np.arange(c)[:, None]
    s = np.arange(c)[None, :]
    blocks = [(s <= t)]
    sel, mask = [], []
    for hs in GLA_LEVELS:
        grp = t // (2 * hs)
        mid = grp * 2 * hs + hs - 1
        upper = (t % (2 * hs)) >= hs
        w = np.where(upper, (s > mid) & (s <= t), (s > t) & (s <= mid))
        blocks.append(w)
        sel.append(np.broadcast_to(upper, (c, c)))
        mask.append(upper & ((s % (2 * hs)) < hs) & ((s // (2 * hs)) == grp))
    wall = np.concatenate(blocks, axis=0).astype(np.float32)
    return (jnp.asarray(wall, BF16), jnp.asarray(np.stack(sel).astype(np.float32)),
            jnp.asarray(np.stack(mask).astype(np.float32)))


def _hgrn_kernel(n_valid, wall_ref, sel_ref, mask_ref, q_ref, lf_ref, k_ref, v_ref, g_ref, gain_ref,
                 o_ref, s_ref, st_scr):
    c = pl.program_id(1)
    heads = st_scr.shape[0]
    tc = q_ref.shape[0]

    @pl.when(c == 0)
    def _():
        st_scr[...] = jnp.zeros_like(st_scr)

    row = c * tc + lax.broadcasted_iota(jnp.int32, (tc, 1), 0)
    valid = row < n_valid
    lf_all = jnp.where(valid, lf_ref[...], 0.0)
    wall = wall_ref[...]
    hi, mid, lo = _split3(lf_all)
    sums = _dot(wall, hi) + _dot(wall, mid) + _dot(wall, lo)

    for h in range(heads):
        sl = slice(h * LANES, (h + 1) * LANES)
        q = q_ref[:, sl]
        k = jnp.where(valid, k_ref[:, sl], 0.0)
        v = v_ref[:, sl]
        vb = v.astype(BF16)
        b = sums[0:tc, sl]
        att = jnp.zeros((tc, tc), F32)
        for li in range(len(GLA_LEVELS)):
            d = sums[(li + 1) * tc:(li + 2) * tc, sl]
            x = (jnp.where(sel_ref[li] > 0.5, q, k) * jnp.exp(d)).astype(BF16)
            att = att + _dot_nt(x, x) * mask_ref[li]
        st = st_scr[h]
        o = (_dot_nt((q * jnp.exp(b)).astype(BF16), st.astype(BF16))
             + _dot(att.astype(BF16), vb)
             + jnp.sum(q * k, axis=-1, keepdims=True) * v)
        b_last = b[tc - 1:tc, :]
        kt = (k * jnp.exp(b_last - b)).astype(BF16)
        st_scr[h] = st * jnp.exp(b_last) + lax.dot_general(vb, kt, TN, preferred_element_type=F32)
        g = g_ref[:, sl]
        o_ref[:, sl] = _rms(o, gain_ref[...]) * _silu(g)

    @pl.when(c == pl.num_programs(1) - 1)
    def _():
        for h in range(heads):
            s_ref[h] = st_scr[h].T


def _hgrn_prompt(q, lf, k, v, g, gain, n_valid, heads):
    bn, lp, w = q.shape
    wall, sel, mask = _gla_constants()
    tok = pl.BlockSpec((None, LANES, w), lambda b, c: (b, c, 0))
    c2 = lambda b, c: (0, 0)
    c3 = lambda b, c: (0, 0, 0)
    dk = w // heads
    return pl.pallas_call(
        functools.partial(_hgrn_kernel, n_valid),
        grid=(bn, lp // LANES),
        in_specs=[pl.BlockSpec(wall.shape, c2), pl.BlockSpec(sel.shape, c3), pl.BlockSpec(mask.shape, c3),
                  tok, tok, tok, tok, tok, pl.BlockSpec((1, dk), c2)],
        out_specs=[tok, pl.BlockSpec((None, heads, dk, dk), lambda b, c: (b, 0, 0, 0))],
        out_shape=[jax.ShapeDtypeStruct((bn, lp, w), F32), jax.ShapeDtypeStruct((bn, heads, dk, dk), F32)],
        scratch_shapes=[pltpu.VMEM((heads, dk, dk), F32)],
        compiler_params=_params("parallel", "arbitrary"),
        name="hgrn_prompt",
    )(wall, sel, mask, q, lf, k, v, g, gain)


def _suffix_constant():
    s = np.arange(LANES)
    tri = (s[:, None] > s[None, :]).astype(np.float32)
    return jnp.asarray(np.concatenate([tri, np.ones((LANES, LANES), np.float32)], axis=1), BF16)


def _sb_tile(q, kj, vj, uo, scale, later, diag):
    z = _dot_nt(q, kj) * scale
    sp = _softplus(z)
    if diag:
        t = lax.broadcasted_iota(jnp.int32, z.shape, 0)
        s = lax.broadcasted_iota(jnp.int32, z.shape, 1)
        vis = s < t
        sp = jnp.where(vis, sp, 0.0)
    hi, lo = _split2(sp)
    res = _dot(hi, uo) + _dot(lo, uo)
    n = z.shape[1]
    a = jnp.exp(z - sp - res[:, :n] - later)
    if diag:
        a = jnp.where(vis, a, 0.0)
    return _dot(a.astype(BF16), vj), later + res[:, n:]


def _sb_kernel(scale, q_ref, k_ref, v_ref, uo_ref, o_ref):
    i = pl.program_id(2)
    q = q_ref[...]
    uo = uo_ref[...]
    tq = q.shape[0]
    off = pl.multiple_of(i * tq, tq)
    acc, later = _sb_tile(q, k_ref[pl.ds(off, tq), :], v_ref[pl.ds(off, tq), :], uo, scale,
                          jnp.zeros((tq, tq), F32), True)

    def body(jj, carry):
        acc, later = carry
        o = pl.multiple_of((i - 1 - jj) * tq, tq)
        d, later = _sb_tile(q, k_ref[pl.ds(o, tq), :], v_ref[pl.ds(o, tq), :], uo, scale, later, False)
        return acc + d, later

    acc, _ = lax.fori_loop(0, i, body, (acc, later))
    o_ref[...] = acc


def _sb_prompt(q, k, v, heads):
    bn, lp, w = q.shape
    hd = w // heads
    uo = _suffix_constant()
    qspec = pl.BlockSpec((None, LANES, hd), lambda b, h, i: (b, i, h))
    kspec = pl.BlockSpec((None, lp, hd), lambda b, h, i: (b, 0, h))
    return pl.pallas_call(
        functools.partial(_sb_kernel, hd ** -0.5),
        grid=(bn, heads, lp // LANES),
        in_specs=[qspec, kspec, kspec, pl.BlockSpec(uo.shape, lambda b, h, i: (0, 0))],
        out_specs=qspec,
        out_shape=jax.ShapeDtypeStruct((bn, lp, w), F32),
        compiler_params=_params("parallel", "parallel", "arbitrary"),
        name="sb_prompt",
    )(q, k, v, uo)


def _even_out_kernel(x_ref, oa_ref, ob_ref, gb_ref, w_ref, y_ref):
    wa = oa_ref.shape[-1]
    ob = ob_ref[...] * _silu(gb_ref[...])
    y_ref[...] = (x_ref[...] + _dot(oa_ref[...].astype(BF16), w_ref[0:wa, :])
                  + _dot(ob.astype(BF16), w_ref[wa:, :]))


def _even_out(x, oa, ob, gb, w, tm):
    bn, ln, d = x.shape
    row = lambda b, j: (b, j, 0)
    wide = pl.BlockSpec((None, tm, d), row)
    half = pl.BlockSpec((None, tm, oa.shape[-1]), row)
    return pl.pallas_call(
        _even_out_kernel,
        grid=(bn, ln // tm),
        in_specs=[wide, half, half, half, pl.BlockSpec(w.shape, lambda b, j: (0, 0))],
        out_specs=wide,
        out_shape=jax.ShapeDtypeStruct(x.shape, F32),
        compiler_params=_params("parallel", "parallel"),
        name="even_out",
    )(x, oa, ob, gb, w)


def _odd_in_kernel(dims, x_ref, ng_ref, w_ref, qag_ref, wuq_ref, kvg_ref, wuk_ref, wuv_ref,
                   gqn_ref, gqr_ref, gkn_ref, gkr_ref, cos_ref, sin_ref,
                   qn_ref, qr_ref, kn_ref, krp_ref, c_ref, kr_ref, v_ref, g_ref):
    ql, kvl, cr, heads = dims
    hb = _rms(x_ref[...], ng_ref[...]).astype(BF16)
    y = _dot(hb, w_ref[...])
    o1, o2 = ql + kvl, ql + kvl + LANES
    cos_t, sin_t = cos_ref[...], sin_ref[...]
    g_ref[...] = y[:, o2:]
    c = _rms(y[:, ql:o1], kvg_ref[...])
    c_ref[...] = c
    kr = _rope(_rms(y[:, o1:o2], gkr_ref[...], cr), cos_t, sin_t, cr // 2)
    kr_ref[...] = kr[:, :cr]
    krp_ref[...] = kr.astype(BF16)
    cb = c.astype(BF16)
    kn = _dot(cb, wuk_ref[...])
    v_ref[...] = _dot(cb, wuv_ref[...]).astype(BF16)
    q = _dot(_rms(y[:, :ql], qag_ref[...]).astype(BF16), wuq_ref[...])
    nw = heads * LANES
    for h in range(heads):
        sl = slice(h * LANES, (h + 1) * LANES)
        kn_ref[:, sl] = _rms(kn[:, sl], gkn_ref[...]).astype(BF16)
        qn_ref[:, sl] = _rms(q[:, sl], gqn_ref[...]).astype(BF16)
        qr = _rms(q[:, nw + h * LANES:nw + (h + 1) * LANES], gqr_ref[...], cr)
        qr_ref[:, sl] = _rope(qr, cos_t, sin_t, cr // 2).astype(BF16)


def _odd_in(x, ng, w, qag, wuq, kvg, wuk, wuv, gqn, gqr, gkn, gkr, cos_t, sin_t, dims, tm):
    bn, ln, d = x.shape
    ql, kvl, cr, heads = dims
    row = lambda b, j: (b, j, 0)
    const = lambda b, j: (0, 0)
    full = lambda a: pl.BlockSpec(a.shape, const)
    nw = heads * LANES
    outs = [(nw, BF16), (nw, BF16), (nw, BF16), (LANES, BF16), (kvl, F32), (cr, F32), (nw, BF16), (nw, F32)]
    return pl.pallas_call(
        functools.partial(_odd_in_kernel, dims),
        grid=(bn, ln // tm),
        in_specs=[pl.BlockSpec((None, tm, d), row), full(ng), full(w), full(qag), full(wuq), full(kvg),
                  full(wuk), full(wuv), full(gqn), full(gqr), full(gkn), full(gkr),
                  pl.BlockSpec((tm, LANES), lambda b, j: (j, 0)), pl.BlockSpec((tm, LANES), lambda b, j: (j, 0))],
        out_specs=[pl.BlockSpec((None, tm, n), row) for n, _ in outs],
        out_shape=[jax.ShapeDtypeStruct((bn, ln, n), dt) for n, dt in outs],
        compiler_params=_params("parallel", "parallel"),
        name="odd_in",
    )(x, ng, w, qag, wuq, kvg, wuk, wuv, gqn, gqr, gkn, gkr, cos_t, sin_t)


def _mla_kernel(scale, qn_ref, qr_ref, kn_ref, kr_ref, v_ref, o_ref):
    i = pl.program_id(2)
    tq = qn_ref.shape[0]
    q = jnp.concatenate([qn_ref[...], qr_ref[...]], axis=1)

    def scores(off):
        kj = jnp.concatenate([kn_ref[pl.ds(off, tq), :], kr_ref[pl.ds(off, tq), :]], axis=1)
        return _dot_nt(q, kj) * scale

    off = pl.multiple_of(i * tq, tq)
    s = scores(off)
    t = lax.broadcasted_iota(jnp.int32, s.shape, 0)
    u = lax.broadcasted_iota(jnp.int32, s.shape, 1)
    s = jnp.where(u <= t, s, -jnp.inf)
    m = jnp.max(s, axis=-1, keepdims=True)
    p = jnp.exp(s - m)
    l = jnp.sum(p, axis=-1, keepdims=True)
    acc = _dot(p.astype(BF16), v_ref[pl.ds(off, tq), :])

    def body(jj, carry):
        m, l, acc = carry
        o = pl.multiple_of(jj * tq, tq)
        s = scores(o)
        m_new = jnp.maximum(m, jnp.max(s, axis=-1, keepdims=True))
        alpha = jnp.exp(m - m_new)
        p = jnp.exp(s - m_new)
        l = alpha * l + jnp.sum(p, axis=-1, keepdims=True)
        acc = alpha * acc + _dot(p.astype(BF16), v_ref[pl.ds(o, tq), :])
        return m_new, l, acc

    m, l, acc = lax.fori_loop(0, i, body, (m, l, acc))
    o_ref[...] = acc / l


def _mla_prompt(qn, qr, kn, krp, v, heads, scale):
    bn, lp, w = qn.shape
    hd = w // heads
    qspec = pl.BlockSpec((None, LANES, hd), lambda b, h, i: (b, i, h))
    kspec = pl.BlockSpec((None, lp, hd), lambda b, h, i: (b, 0, h))
    return pl.pallas_call(
        functools.partial(_mla_kernel, scale),
        grid=(bn, heads, lp // LANES),
        in_specs=[qspec, qspec, kspec, pl.BlockSpec((None, lp, LANES), lambda b, h, i: (b, 0, 0)), kspec],
        out_specs=qspec,
        out_shape=jax.ShapeDtypeStruct((bn, lp, w), F32),
        compiler_params=_params("parallel", "parallel", "arbitrary"),
        name="mla_prompt",
    )(qn, qr, kn, krp, v)


def _odd_out_kernel(x_ref, o_ref, g_ref, w_ref, y_ref):
    og = (o_ref[...] * _silu(g_ref[...])).astype(BF16)
    y_ref[...] = x_ref[...] + _dot(og, w_ref[...])


def _odd_out(x, o, g, w, tm):
    bn, ln, d = x.shape
    row = lambda b, j: (b, j, 0)
    wide = pl.BlockSpec((None, tm, d), row)
    return pl.pallas_call(
        _odd_out_kernel,
        grid=(bn, ln // tm),
        in_specs=[wide, pl.BlockSpec((None, tm, o.shape[-1]), row), pl.BlockSpec((None, tm, g.shape[-1]), row),
                  pl.BlockSpec(w.shape, lambda b, j: (0, 0))],
        out_specs=wide,
        out_shape=jax.ShapeDtypeStruct(x.shape, F32),
        compiler_params=_params("parallel", "parallel"),
        name="odd_out",
    )(x, o, g, w)


def _odd_out_dec_kernel(heads, x_ref, ctx_ref, g_ref, wuv_ref, w_ref, y_ref):
    kvl = ctx_ref.shape[-1] // heads
    vd = wuv_ref.shape[-1] // heads
    y = x_ref[...]
    for h in range(heads):
        o = _dot(ctx_ref[:, h * kvl:(h + 1) * kvl].astype(BF16), wuv_ref[:, h * vd:(h + 1) * vd])
        og = (o * _silu(g_ref[:, h * vd:(h + 1) * vd])).astype(BF16)
        y = y + _dot(og, w_ref[h * vd:(h + 1) * vd, :])
    y_ref[...] = y


def _odd_out_dec(x, ctx, g, wuv, w, heads):
    full = lambda a: pl.BlockSpec(a.shape, lambda: (0,) * a.ndim)
    return pl.pallas_call(
        functools.partial(_odd_out_dec_kernel, heads),
        in_specs=[full(x), full(ctx), full(g), full(wuv), full(w)],
        out_specs=full(x),
        out_shape=jax.ShapeDtypeStruct(x.shape, F32),
        compiler_params=pltpu.CompilerParams(vmem_limit_bytes=VMEM_LIMIT),
        name="odd_out_dec",
    )(x, ctx, g, wuv, w)


def _column(row):
    n = row.shape[1]
    r = lax.broadcasted_iota(jnp.int32, (n, n), 0)
    c = lax.broadcasted_iota(jnp.int32, (n, n), 1)
    return jnp.sum(jnp.where(r == c, jnp.broadcast_to(row, (n, n)), 0.0), axis=1, keepdims=True)


def _hgrn_dec_kernel(q_ref, lf_ref, k_ref, v_ref, g_ref, gain_ref, s0_ref, o_ref, s_ref):
    heads = s0_ref.shape[0]
    for h in range(heads):
        sl = slice(h * LANES, (h + 1) * LANES)
        q, lf, k, v = q_ref[:, sl], lf_ref[:, sl], k_ref[:, sl], v_ref[:, sl]
        s0 = s0_ref[h]
        decay = jnp.exp(lf)
        qe = jnp.broadcast_to(q * decay, (8, LANES)).astype(BF16)
        o = (_dot(qe, s0.astype(BF16))[0:1] + jnp.sum(q * k, axis=-1, keepdims=True) * v)
        s_ref[h] = _column(decay) * s0 + _column(k) * v
        o_ref[:, sl] = _rms(o, gain_ref[...]) * _silu(g_ref[:, sl])


def _hgrn_decode(q, lf, k, v, g, gain, state, e):
    db, _, w = q.shape
    heads, dk, dv = state.shape[2:]
    tok = pl.BlockSpec((None, 1, w), lambda b: (b, 0, 0))
    return pl.pallas_call(
        _hgrn_dec_kernel,
        grid=(db,),
        in_specs=[tok, tok, tok, tok, tok, pl.BlockSpec((1, dv), lambda b: (0, 0)),
                  pl.BlockSpec((None, None, heads, dk, dv), lambda b: (e, b, 0, 0, 0))],
        out_specs=[tok, pl.BlockSpec((None, heads, dk, dv), lambda b: (b, 0, 0, 0))],
        out_shape=[jax.ShapeDtypeStruct((db, 1, w), F32), jax.ShapeDtypeStruct((db, heads, dk, dv), F32)],
        compiler_params=_params("parallel"),
        name="hgrn_decode",
    )(q, lf, k, v, g, gain, state)


def _head_rows(row, heads):
    n = row.shape[1]
    r = lax.broadcasted_iota(jnp.int32, (8, n), 0)
    c = lax.broadcasted_iota(jnp.int32, (8, n), 1)
    return jnp.where(c // (n // heads) == r, jnp.broadcast_to(row, (8, n)), 0.0)


def _sb_dec_kernel(scale, heads, pg, pt_ref, q_ref, uo_ref, *refs):
    k_refs, v_refs = refs[:pg], refs[pg:2 * pg]
    o_ref, acc_scr, later_scr = refs[2 * pg:]
    g = pl.program_id(1)

    @pl.when(g == 0)
    def _():
        acc_scr[...] = jnp.zeros_like(acc_scr)
        later_scr[...] = jnp.zeros_like(later_scr)

    qm = _head_rows(q_ref[...].astype(F32), heads).astype(BF16)
    uo = uo_ref[...]
    acc = acc_scr[...]
    later = later_scr[...]
    for r in range(pg - 1, -1, -1):
        z = _dot_nt(qm, k_refs[r][...].astype(BF16)) * scale
        sp = _softplus(z)
        hi, lo = _split2(sp)
        res = _dot(jnp.concatenate([hi, lo], axis=0), uo)
        n = z.shape[1]
        a = jnp.exp(z - sp - (res[0:8, :n] + res[8:16, :n]) - later)
        acc = acc + _dot(a.astype(BF16), v_refs[r][...].astype(BF16))
        later = later + res[0:8, n:] + res[8:16, n:]
    acc_scr[...] = acc
    later_scr[...] = later

    @pl.when(g == pl.num_programs(1) - 1)
    def _():
        n = acc.shape[1]
        r = lax.broadcasted_iota(jnp.int32, (8, n), 0)
        c = lax.broadcasted_iota(jnp.int32, (8, n), 1)
        o_ref[...] = jnp.sum(jnp.where(c // (n // heads) == r, acc, 0.0), axis=0, keepdims=True)


def _sb_decode(q, cache_k, cache_v, page_table, e, heads, pg):
    db, _, w = q.shape
    n_pages = page_table.shape[1]
    ps = cache_k.shape[2]
    ng = n_pages // pg
    uo = _suffix_constant()

    def page_spec(r):
        return pl.BlockSpec((None, None, ps, w), lambda b, g, pt: (e, pt[b, (ng - 1 - g) * pg + r], 0, 0))

    tok = pl.BlockSpec((None, 1, w), lambda b, g, pt: (b, 0, 0))
    grid_spec = pltpu.PrefetchScalarGridSpec(
        num_scalar_prefetch=1,
        grid=(db, ng),
        in_specs=[tok, pl.BlockSpec(uo.shape, lambda b, g, pt: (0, 0))]
                 + [page_spec(r) for r in range(pg)] * 2,
        out_specs=tok,
        scratch_shapes=[pltpu.VMEM((8, w), F32), pltpu.VMEM((8, ps), F32)],
    )
    return pl.pallas_call(
        functools.partial(_sb_dec_kernel, (w // heads) ** -0.5, heads, pg),
        grid_spec=grid_spec,
        out_shape=jax.ShapeDtypeStruct((db, 1, w), F32),
        compiler_params=_params("parallel", "arbitrary"),
        name="sb_decode",
    )(page_table, q, uo, *([cache_k] * pg), *([cache_v] * pg))


def _mla_dec_kernel(scale, heads, pg, cr, pt_ref, qn_ref, qr_ref, knn_ref, krn_ref, cn_ref, gkn_ref,
                    wuk_ref, wukt_ref, *refs):
    c_refs, kr_refs = refs[:pg], refs[pg:2 * pg]
    ctx_ref, lhs_scr, qr_scr, cbf_scr, m_scr, l_scr, acc_scr = refs[2 * pg:]
    g = pl.program_id(1)
    nk = wukt_ref.shape[0]
    nope = nk // heads
    kvl = wukt_ref.shape[1]

    @pl.when(g == 0)
    def _():
        lhs_scr[0:nk, :] = wukt_ref[...]
        qg = qn_ref[...].astype(F32)
        self_s = jnp.zeros((8, LANES), F32)
        rowid = lax.broadcasted_iota(jnp.int32, (8, LANES), 0)
        absorbed = jnp.zeros((8, kvl), F32)
        arow = lax.broadcasted_iota(jnp.int32, (8, kvl), 0)
        for h in range(heads):
            sl = slice(h * nope, (h + 1) * nope)
            qh = qg[:, sl]
            qt = jnp.broadcast_to(qh * gkn_ref[...], (8, nope)).astype(BF16)
            u = _dot_nt(qt, wuk_ref[:, sl])
            absorbed = jnp.where(arow == h, u, absorbed)
            qr_h = qr_ref[:, h * LANES:(h + 1) * LANES].astype(F32)
            qr_scr[h:h + 1, :] = qr_h
            s_h = (jnp.sum(qh * knn_ref[:, sl].astype(F32), axis=-1, keepdims=True)
                   + jnp.sum(qr_h * krn_ref[...].astype(F32), axis=-1, keepdims=True)) * scale
            self_s = jnp.where(rowid == h, jnp.broadcast_to(s_h, (8, LANES)), self_s)
        lhs_scr[nk:nk + 16, :] = jnp.concatenate([absorbed, jnp.zeros_like(absorbed)], axis=0).astype(BF16)
        m_scr[...] = self_s
        l_scr[...] = jnp.ones_like(l_scr)
        acc_scr[...] = jnp.broadcast_to(cn_ref[...], acc_scr.shape)

    ps = c_refs[0].shape[0]
    for r in range(pg):
        cbf_scr[r * ps:(r + 1) * ps, :] = c_refs[r][...].astype(BF16)
    qr8 = qr_scr[...][:, :cr].astype(BF16)
    lhs = lhs_scr[...]
    m, l, acc = m_scr[...], l_scr[...], acc_scr[...]
    width = 2 * ps
    for t in range(pg * ps // width):
        cb = cbf_scr[t * width:(t + 1) * width, :]
        kt = _dot_nt(lhs, cb)
        ssq = jnp.sum((kt[0:nk] * kt[0:nk]).reshape(heads, nope, width), axis=1)
        krb = jnp.concatenate([kr_refs[2 * t][...], kr_refs[2 * t + 1][...]], axis=0).astype(BF16)
        s = (kt[nk:nk + 8] * lax.rsqrt(ssq / nope + EPS) + _dot_nt(qr8, krb)) * scale
        m_new = jnp.maximum(m, jnp.max(s, axis=-1, keepdims=True))
        alpha = jnp.exp(m - m_new)
        p = jnp.exp(s - m_new[:, 0:1])
        l = alpha * l + jnp.sum(p, axis=-1, keepdims=True)
        acc = alpha[:, 0:1] * acc + _dot(p.astype(BF16), cb)
        m = m_new
    m_scr[...], l_scr[...], acc_scr[...] = m, l, acc

    @pl.when(g == pl.num_programs(1) - 1)
    def _():
        ctx_ref[...] = acc / l[:, 0:1]


def _mla_decode(qn, qr, knn, krn, cn, gkn, wuk, wukt, cache_c, cache_kr, page_table, o, heads, pg, scale):
    db = qn.shape[0]
    n_pages = page_table.shape[1]
    ps, kvl = cache_c.shape[2:]
    cr = cache_kr.shape[3]
    ng = n_pages // pg
    nk = wukt.shape[0]

    def page_spec(r, n):
        return pl.BlockSpec((None, None, ps, n), lambda b, g, pt: (o, pt[b, g * pg + r], 0, 0))

    tok = lambda a: pl.BlockSpec((None, 1, a.shape[-1]), lambda b, g, pt: (b, 0, 0))
    const = lambda a: pl.BlockSpec(a.shape, lambda b, g, pt: (0, 0))
    grid_spec = pltpu.PrefetchScalarGridSpec(
        num_scalar_prefetch=1,
        grid=(db, ng),
        in_specs=[tok(qn), tok(qr), tok(knn), tok(krn), tok(cn), const(gkn), const(wuk), const(wukt)]
                 + [page_spec(r, kvl) for r in range(pg)] + [page_spec(r, cr) for r in range(pg)],
        out_specs=pl.BlockSpec((None, 8, kvl), lambda b, g, pt: (b, 0, 0)),
        scratch_shapes=[pltpu.VMEM((nk + 16, kvl), BF16), pltpu.VMEM((8, LANES), F32),
                        pltpu.VMEM((pg * ps, kvl), BF16), pltpu.VMEM((8, LANES), F32),
                        pltpu.VMEM((8, LANES), F32), pltpu.VMEM((8, kvl), F32)],
    )
    return pl.pallas_call(
        functools.partial(_mla_dec_kernel, scale, heads, pg, cr),
        grid_spec=grid_spec,
        out_shape=jax.ShapeDtypeStruct((db, 8, kvl), F32),
        compiler_params=_params("parallel", "arbitrary"),
        name="mla_decode",
    )(page_table, qn, qr, knn, krn, cn, gkn, wuk, wukt, *([cache_c] * pg), *([cache_kr] * pg))


def _rope_tables(pos, cr):
    half = cr // 2
    inv = ROPE_THETA ** (-jnp.arange(half, dtype=F32) / half)
    ang = pos.astype(F32)[:, None] * inv[None, :]
    cos, sin = jnp.cos(ang), jnp.sin(ang)
    pad = LANES - cr
    n = pos.shape[0]
    return (jnp.concatenate([cos, cos, jnp.ones((n, pad), F32)], axis=1),
            jnp.concatenate([-sin, sin, jnp.zeros((n, pad), F32)], axis=1))


def _pad_lanes(a, width):
    return jnp.pad(a, [(0, 0)] * (a.ndim - 1) + [(0, width - a.shape[-1])])


def kernel(x_prompt, x_sample, state_hgrn, cache_sb_k, cache_sb_v, cache_mla_ckv, cache_mla_krope, page_table,
           meta_tokens, norm_gain, lb_logits, w_in_even, hgrn_out_gain, w_out_even, w_in_odd, q_a_gain, w_uq,
           kv_a_gain, w_uk, w_uv, qk_gain_q_nope, qk_gain_q_rope, qk_gain_k_nope, qk_gain_k_rope, w_out_odd):
    bp, seq, d = x_prompt.shape
    db, dec_t, _ = x_sample.shape
    assert dec_t == 1, "the sample-group kernels handle exactly one new token per sequence"
    n_meta = meta_tokens.shape[0]
    n_tok = seq + n_meta
    lp = -(-n_tok // LANES) * LANES
    depth = norm_gain.shape[0]
    a_heads, a_kd, a_vd = state_hgrn.shape[2:]
    n_phys, ps, b_heads, b_hd = cache_sb_k.shape[1:]
    assert a_kd == LANES and a_vd == LANES and b_hd == LANES and ps == LANES
    n_pages = page_table.shape[1]
    past = n_pages * ps
    ql, kvl = q_a_gain.shape[1], kv_a_gain.shape[1]
    c_nope, c_rope = qk_gain_q_nope.shape[1], qk_gain_q_rope.shape[1]
    c_heads = w_uk.shape[2] // c_nope
    assert c_nope == LANES and w_uv.shape[2] == c_heads * LANES and c_heads == 8 and c_rope <= LANES
    assert ql % LANES == 0 and kvl % LANES == 0
    mla_scale = (c_nope + c_rope) ** -0.5
    dims = (ql, kvl, c_rope, c_heads)
    tm = lp // 4 if (lp // 4) % 16 == 0 else LANES
    pg = 8 if n_pages % 8 == 0 else 2
    assert n_pages % pg == 0

    w_in_e = w_in_even.astype(BF16)
    w_out_e = w_out_even.astype(BF16)
    w_in_o = jnp.concatenate([w_in_odd[..., :ql + kvl], _pad_lanes(w_in_odd[..., ql + kvl:ql + kvl + c_rope], LANES),
                              w_in_odd[..., ql + kvl + c_rope:]], axis=-1).astype(BF16)
    wq = w_uq.reshape(w_uq.shape[0], ql, c_heads, c_nope + c_rope)
    w_uq_p = jnp.concatenate([wq[..., :c_nope].reshape(-1, ql, c_heads * c_nope),
                              _pad_lanes(wq[..., c_nope:], LANES).reshape(-1, ql, c_heads * LANES)],
                             axis=-1).astype(BF16)
    w_uk_b = w_uk.astype(BF16)
    w_uk_t = jnp.swapaxes(w_uk, 1, 2).astype(BF16)
    w_uv_b = w_uv.astype(BF16)
    w_out_o = w_out_odd.astype(BF16)
    g_qr = _pad_lanes(qk_gain_q_rope, LANES)
    g_kr = _pad_lanes(qk_gain_k_rope, LANES)

    def rowof(a, i):
        return a[i:i + 1]

    meta = jnp.broadcast_to(meta_tokens[None].astype(x_prompt.dtype), (bp, n_meta, d))
    x = jnp.concatenate([meta, x_prompt, jnp.zeros((bp, lp - n_tok, d), x_prompt.dtype)], axis=1)
    cos_p, sin_p = _rope_tables(jnp.arange(lp, dtype=jnp.int32), c_rope)
    p_s, p_k, p_v, p_c, p_kr = [], [], [], [], []
    for layer in range(depth):
        i = layer // 2
        ng = rowof(norm_gain, layer)
        if layer % 2 == 0:
            qa, lf, ka, ia, ga, qb, kb, vb, kb16, vb16, gb = _even_in(x, ng, lb_logits, w_in_e[i], layer, tm)
            oa, s_new = _hgrn_prompt(qa, lf, ka, ia, ga, rowof(hgrn_out_gain, i), n_tok, a_heads)
            ob = _sb_prompt(qb, kb16, vb16, b_heads)
            x = _even_out(x, oa, ob, gb, w_out_e[i], tm)
            p_s.append(s_new)
            p_k.append(kb[:, :n_tok].reshape(bp, n_tok, b_heads, b_hd))
            p_v.append(vb[:, :n_tok].reshape(bp, n_tok, b_heads, b_hd))
        else:
            qn, qr, kn, krp, c, kr, v, g = _odd_in(
                x, ng, w_in_o[i], rowof(q_a_gain, i), w_uq_p[i], rowof(kv_a_gain, i), w_uk_b[i], w_uv_b[i],
                rowof(qk_gain_q_nope, i), rowof(g_qr, i), rowof(qk_gain_k_nope, i), rowof(g_kr, i),
                cos_p, sin_p, dims, tm)
            o = _mla_prompt(qn, qr, kn, krp, v, c_heads, mla_scale)
            x = _odd_out(x, o, g, w_out_o[i], tm)
            p_c.append(c[:, :n_tok])
            p_kr.append(kr[:, :n_tok])
    y_prompt = x[:, n_meta:n_tok]

    xs = x_sample.reshape(1, db, d)
    cos_s, sin_s = _rope_tables(jnp.full((db,), past, jnp.int32), c_rope)
    sb_k = cache_sb_k.reshape(cache_sb_k.shape[0], n_phys, ps, b_heads * b_hd)
    sb_v = cache_sb_v.reshape(cache_sb_v.shape[0], n_phys, ps, b_heads * b_hd)
    tokens = lambda a: a.reshape(db, 1, a.shape[-1])
    s_s, s_k, s_v, s_c, s_kr = [], [], [], [], []
    for layer in range(depth):
        i = layer // 2
        ng = rowof(norm_gain, layer)
        if layer % 2 == 0:
            qa, lf, ka, ia, ga, qb, kb, vb, _, _, gb = _even_in(xs, ng, lb_logits, w_in_e[i], layer, db)
            oa, s_new = _hgrn_decode(tokens(qa), tokens(lf), tokens(ka), tokens(ia), tokens(ga),
                                     rowof(hgrn_out_gain, i), state_hgrn, i)
            ob = _sb_decode(tokens(qb), sb_k, sb_v, page_table, i, b_heads, pg)
            xs = _even_out(xs, oa.reshape(1, db, -1), ob.reshape(1, db, -1), gb, w_out_e[i], db)
            s_s.append(s_new)
            s_k.append(kb.reshape(db, 1, b_heads, b_hd))
            s_v.append(vb.reshape(db, 1, b_heads, b_hd))
        else:
            qn, qr, kn, krp, c, kr, _, g = _odd_in(
                xs, ng, w_in_o[i], rowof(q_a_gain, i), w_uq_p[i], rowof(kv_a_gain, i), w_uk_b[i], w_uv_b[i],
                rowof(qk_gain_q_nope, i), rowof(g_qr, i), rowof(qk_gain_k_nope, i), rowof(g_kr, i),
                cos_s, sin_s, dims, db)
            ctx = _mla_decode(tokens(qn), tokens(qr), tokens(kn), tokens(krp), tokens(c),
                              rowof(qk_gain_k_nope, i), w_uk_b[i], w_uk_t[i], cache_mla_ckv, cache_mla_krope,
                              page_table, i, c_heads, pg, mla_scale)
            ctx = ctx[:, :c_heads].reshape(db, c_heads * kvl)
            xs = _odd_out_dec(xs[0], ctx, g[0], w_uv_b[i], w_out_o[i], c_heads)[None]
            s_c.append(c.reshape(db, 1, kvl))
            s_kr.append(kr.reshape(db, 1, c_rope))
    y_sample = xs.reshape(db, 1, d)

    return (y_prompt, y_sample, jnp.stack(p_s), jnp.stack(p_k), jnp.stack(p_v), jnp.stack(p_c), jnp.stack(p_kr),
            jnp.stack(s_s), jnp.stack(s_k), jnp.stack(s_v), jnp.stack(s_c), jnp.stack(s_kr))
```

```python
import functools

import numpy as np
import jax
import jax.numpy as jnp
from jax import lax
from jax.experimental import pallas as pl
from jax.experimental.pallas import tpu as pltpu

EPS = 1e-6
ROPE_THETA = 10000.0
LANES = 128
VMEM_LIMIT = 56 * 1024 * 1024
F32 = jnp.float32
BF16 = jnp.bfloat16
GLA_LEVELS = (64, 32, 16, 8, 4, 2, 1)

NT = (((1,), (1,)), ((), ()))
TN = (((0,), (0,)), ((), ()))


def _params(*sem):
    return pltpu.CompilerParams(dimension_semantics=sem, vmem_limit_bytes=VMEM_LIMIT)


def _dot(a, b):
    return jnp.dot(a, b, preferred_element_type=F32)


def _dot_nt(a, b):
    return lax.dot_general(a, b, NT, preferred_element_type=F32)


def _rms(x, gain, n=None):
    ss = jnp.sum(x * x, axis=-1, keepdims=True)
    ms = ss / (x.shape[-1] if n is None else n)
    return x * lax.rsqrt(ms + EPS) * gain


def _sigmoid(x):
    return 1.0 / (1.0 + jnp.exp(-x))


def _silu(x):
    return x * _sigmoid(x)


def _softplus(z):
    return jnp.maximum(z, 0.0) + jnp.log1p(jnp.exp(-jnp.abs(z)))


def _split2(x):
    hi = x.astype(BF16)
    lo = (x - hi.astype(F32)).astype(BF16)
    return hi, lo


def _split3(x):
    hi = x.astype(BF16)
    r = x - hi.astype(F32)
    mid = r.astype(BF16)
    lo = (r - mid.astype(F32)).astype(BF16)
    return hi, mid, lo


def _rope(x, cos_t, sin_t, half):
    lane = lax.broadcasted_iota(jnp.int32, x.shape, 1)
    swapped = jnp.where(lane < half, pltpu.roll(x, LANES - half, 1), pltpu.roll(x, half, 1))
    return x * cos_t + swapped * sin_t


def _even_in_kernel(layer, x_ref, ng_ref, lbl_ref, w_ref,
                    qa_ref, lf_ref, ka_ref, ia_ref, ga_ref, qb_ref, kb_ref, vb_ref, kb16_ref, vb16_ref, gb_ref):
    gw = qa_ref.shape[-1]
    hb = _rms(x_ref[...], ng_ref[...]).astype(BF16)

    def proj(g):
        return _dot(hb, w_ref[:, g * gw:(g + 1) * gw])

    logits = lbl_ref[...]
    e = jnp.exp(logits - jnp.max(logits, axis=0, keepdims=True))
    sm = e / jnp.sum(e, axis=0, keepdims=True)
    cum = sm[0:1]
    for i in range(1, layer + 1):
        cum = cum + sm[i:i + 1]
    lb = cum - sm[0:1]

    qa_ref[...] = proj(0)
    f = lb + (1.0 - lb) * _sigmoid(proj(1))
    lf_ref[...] = jnp.log(f)
    ka_ref[...] = 1.0 - f
    ia_ref[...] = proj(2)
    ga_ref[...] = proj(3)
    qb_ref[...] = proj(4).astype(BF16)
    kb = proj(5)
    kb_ref[...] = kb
    kb16_ref[...] = kb.astype(BF16)
    vb = proj(6)
    vb_ref[...] = vb
    vb16_ref[...] = vb.astype(BF16)
    gb_ref[...] = proj(7)


def _even_in(x, ng, lbl, w, layer, tm):
    bn, ln, d = x.shape
    gw = w.shape[1] // 8
    row = lambda b, j: (b, j, 0)
    const = lambda b, j: (0, 0)
    out_dt = [F32, F32, F32, F32, F32, BF16, F32, F32, BF16, BF16, F32]
    return pl.pallas_call(
        functools.partial(_even_in_kernel, layer),
        grid=(bn, ln // tm),
        in_specs=[pl.BlockSpec((None, tm, d), row), pl.BlockSpec((1, d), const),
                  pl.BlockSpec(lbl.shape, const), pl.BlockSpec(w.shape, const)],
        out_specs=[pl.BlockSpec((None, tm, gw), row)] * len(out_dt),
        out_shape=[jax.ShapeDtypeStruct((bn, ln, gw), dt) for dt in out_dt],
        compiler_params=_params("parallel", "parallel"),
        name="even_in",
    )(x, ng, lbl, w)


def _gla_constants():
    c = LANES
    t = np.arange(c)[:, None]
    s = np.arange(c)[None, :]
    blocks = [(s <= t)]
    sel, mask = [], []
    for hs in GLA_LEVELS:
        grp = t // (2 * hs)
        mid = grp * 2 * hs + hs - 1
        upper = (t % (2 * hs)) >= hs
        w = np.where(upper, (s > mid) & (s <= t), (s > t) & (s <= mid))
        blocks.append(w)
        sel.append(np.broadcast_to(upper, (c, c)))
        mask.append(upper & ((s % (2 * hs)) < hs) & ((s // (2 * hs)) == grp))
    wall = np.concatenate(blocks, axis=0).astype(np.float32)
    return (jnp.asarray(wall, BF16), jnp.asarray(np.stack(sel).astype(np.float32)),
            jnp.asarray(np.stack(mask).astype(np.float32)))


def _hgrn_kernel(n_valid, wall_ref, sel_ref, mask_ref, q_ref, lf_ref, k_ref, v_ref, g_ref, gain_ref,
                 o_ref, s_ref, st_scr):
    c = pl.program_id(1)
    heads = st_scr.shape[0]
    tc = q_ref.shape[0]

    @pl.when(c == 0)
    def _():
        st_scr[...] = jnp.zeros_like(st_scr)

    row = c * tc + lax.broadcasted_iota(jnp.int32, (tc, 1), 0)
    valid = row < n_valid
    lf_all = jnp.where(valid, lf_ref[...], 0.0)
    wall = wall_ref[...]
    hi, mid, lo = _split3(lf_all)
    sums = _dot(wall, hi) + _dot(wall, mid) + _dot(wall, lo)

    for h in range(heads):
        sl = slice(h * LANES, (h + 1) * LANES)
        q = q_ref[:, sl]
        k = jnp.where(valid, k_ref[:, sl], 0.0)
        v = v_ref[:, sl]
        vb = v.astype(BF16)
        b = sums[0:tc, sl]
        att = jnp.zeros((tc, tc), F32)
        for li in range(len(GLA_LEVELS)):
            d = sums[(li + 1) * tc:(li + 2) * tc, sl]
            x = (jnp.where(sel_ref[li] > 0.5, q, k) * jnp.exp(d)).astype(BF16)
            att = att + _dot_nt(x, x) * mask_ref[li]
        st = st_scr[h]
        o = (_dot_nt((q * jnp.exp(b)).astype(BF16), st.astype(BF16))
             + _dot(att.astype(BF16), vb)
             + jnp.sum(q * k, axis=-1, keepdims=True) * v)
        b_last = b[tc - 1:tc, :]
        kt = (k * jnp.exp(b_last - b)).astype(BF16)
        st_scr[h] = st * jnp.exp(b_last) + lax.dot_general(vb, kt, TN, preferred_element_type=F32)
        g = g_ref[:, sl]
        o_ref[:, sl] = _rms(o, gain_ref[...]) * _silu(g)

    @pl.when(c == pl.num_programs(1) - 1)
    def _():
        for h in range(heads):
            s_ref[h] = st_scr[h].T


def _hgrn_prompt(q, lf, k, v, g, gain, n_valid, heads):
    bn, lp, w = q.shape
    wall, sel, mask = _gla_constants()
    tok = pl.BlockSpec((None, LANES, w), lambda b, c: (b, c, 0))
    c2 = lambda b, c: (0, 0)
    c3 = lambda b, c: (0, 0, 0)
    dk = w // heads
    return pl.pallas_call(
        functools.partial(_hgrn_kernel, n_valid),
        grid=(bn, lp // LANES),
        in_specs=[pl.BlockSpec(wall.shape, c2), pl.BlockSpec(sel.shape, c3), pl.BlockSpec(mask.shape, c3),
                  tok, tok, tok, tok, tok, pl.BlockSpec((1, dk), c2)],
        out_specs=[tok, pl.BlockSpec((None, heads, dk, dk), lambda b, c: (b, 0, 0, 0))],
        out_shape=[jax.ShapeDtypeStruct((bn, lp, w), F32), jax.ShapeDtypeStruct((bn, heads, dk, dk), F32)],
        scratch_shapes=[pltpu.VMEM((heads, dk, dk), F32)],
        compiler_params=_params("parallel", "arbitrary"),
        name="hgrn_prompt",
    )(wall, sel, mask, q, lf, k, v, g, gain)


def _suffix_constant():
    s = np.arange(LANES)
    tri = (s[:, None] > s[None, :]).astype(np.float32)
    return jnp.asarray(np.concatenate([tri, np.ones((LANES, LANES), np.float32)], axis=1), BF16)


def _sb_tile(scale, heads, q_ref, k_ref, v_ref, uo, later_scr, acc_scr, off, first):
    tq = q_ref.shape[0]
    hd = q_ref.shape[1] // heads
    hs = [slice(h * hd, (h + 1) * hd) for h in range(heads)]
    zs = [_dot_nt(q_ref[:, sl], k_ref[pl.ds(off, tq), sl]) * scale for sl in hs]
    sps = [_softplus(z) for z in zs]
    if first:
        t = lax.broadcasted_iota(jnp.int32, (tq, tq), 0)
        s = lax.broadcasted_iota(jnp.int32, (tq, tq), 1)
        vis = s < t
        sps = [jnp.where(vis, sp, 0.0) for sp in sps]
    res = _dot(jnp.concatenate([piece for sp in sps for piece in _split2(sp)], axis=0), uo)
    for h, sl in enumerate(hs):
        r = res[2 * h * tq:(2 * h + 1) * tq] + res[(2 * h + 1) * tq:(2 * h + 2) * tq]
        log_a = zs[h] - sps[h] - r[:, :tq]
        if first:
            a = jnp.where(vis, jnp.exp(log_a), 0.0)
            later_scr[h] = r[:, tq:]
            acc_scr[:, sl] = _dot(a.astype(BF16), v_ref[pl.ds(off, tq), sl])
        else:
            later = later_scr[h]
            a = jnp.exp(log_a - later)
            later_scr[h] = later + r[:, tq:]
            acc_scr[:, sl] += _dot(a.astype(BF16), v_ref[pl.ds(off, tq), sl])


def _sb_kernel(scale, heads, q_ref, k_ref, v_ref, uo_ref, o_ref, later_scr, acc_scr):
    i = pl.program_id(1)
    tq = q_ref.shape[0]
    uo = uo_ref[...]
    tile = functools.partial(_sb_tile, scale, heads, q_ref, k_ref, v_ref, uo, later_scr, acc_scr)
    tile(pl.multiple_of(i * tq, tq), True)

    def body(jj, carry):
        tile(pl.multiple_of((i - 1 - jj) * tq, tq), False)
        return carry

    lax.fori_loop(0, i, body, 0)
    o_ref[...] = acc_scr[...]


def _sb_prompt(q, k, v, heads):
    bn, lp, w = q.shape
    uo = _suffix_constant()
    qspec = pl.BlockSpec((None, LANES, w), lambda b, i: (b, i, 0))
    kspec = pl.BlockSpec((None, lp, w), lambda b, i: (b, 0, 0))
    return pl.pallas_call(
        functools.partial(_sb_kernel, (w // heads) ** -0.5, heads),
        grid=(bn, lp // LANES),
        in_specs=[qspec, kspec, kspec, pl.BlockSpec(uo.shape, lambda b, i: (0, 0))],
        out_specs=qspec,
        out_shape=jax.ShapeDtypeStruct((bn, lp, w), F32),
        scratch_shapes=[pltpu.VMEM((heads, LANES, LANES), F32), pltpu.VMEM((LANES, w), F32)],
        compiler_params=_params("parallel", "arbitrary"),
        name="sb_prompt",
    )(q, k, v, uo)


def _even_out_kernel(x_ref, oa_ref, ob_ref, gb_ref, w_ref, y_ref):
    wa = oa_ref.shape[-1]
    ob = ob_ref[...] * _silu(gb_ref[...])
    y_ref[...] = (x_ref[...] + _dot(oa_ref[...].astype(BF16), w_ref[0:wa, :])
                  + _dot(ob.astype(BF16), w_ref[wa:, :]))


def _even_out(x, oa, ob, gb, w, tm):
    bn, ln, d = x.shape
    row = lambda b, j: (b, j, 0)
    wide = pl.BlockSpec((None, tm, d), row)
    half = pl.BlockSpec((None, tm, oa.shape[-1]), row)
    return pl.pallas_call(
        _even_out_kernel,
        grid=(bn, ln // tm),
        in_specs=[wide, half, half, half, pl.BlockSpec(w.shape, lambda b, j: (0, 0))],
        out_specs=wide,
        out_shape=jax.ShapeDtypeStruct(x.shape, F32),
        compiler_params=_params("parallel", "parallel"),
        name="even_out",
    )(x, oa, ob, gb, w)


def _odd_in_kernel(dims, x_ref, ng_ref, w_ref, qag_ref, wuq_ref, kvg_ref, wuk_ref, wuv_ref,
                   gqn_ref, gqr_ref, gkn_ref, gkr_ref, cos_ref, sin_ref,
                   qn_ref, qr_ref, kn_ref, krp_ref, c_ref, kr_ref, v_ref, g_ref):
    ql, kvl, cr, heads = dims
    hb = _rms(x_ref[...], ng_ref[...]).astype(BF16)
    y = _dot(hb, w_ref[...])
    o1, o2 = ql + kvl, ql + kvl + LANES
    cos_t, sin_t = cos_ref[...], sin_ref[...]
    g_ref[...] = y[:, o2:]
    c = _rms(y[:, ql:o1], kvg_ref[...])
    c_ref[...] = c
    kr = _rope(_rms(y[:, o1:o2], gkr_ref[...], cr), cos_t, sin_t, cr // 2)
    kr_ref[...] = kr[:, :cr]
    krp_ref[...] = kr.astype(BF16)
    cb = c.astype(BF16)
    kn = _dot(cb, wuk_ref[...])
    v_ref[...] = _dot(cb, wuv_ref[...]).astype(BF16)
    q = _dot(_rms(y[:, :ql], qag_ref[...]).astype(BF16), wuq_ref[...])
    nw = heads * LANES
    for h in range(heads):
        sl = slice(h * LANES, (h + 1) * LANES)
        kn_ref[:, sl] = _rms(kn[:, sl], gkn_ref[...]).astype(BF16)
        qn_ref[:, sl] = _rms(q[:, sl], gqn_ref[...]).astype(BF16)
        qr = _rms(q[:, nw + h * LANES:nw + (h + 1) * LANES], gqr_ref[...], cr)
        qr_ref[:, sl] = _rope(qr, cos_t, sin_t, cr // 2).astype(BF16)


def _odd_in(x, ng, w, qag, wuq, kvg, wuk, wuv, gqn, gqr, gkn, gkr, cos_t, sin_t, dims, tm):
    bn, ln, d = x.shape
    ql, kvl, cr, heads = dims
    row = lambda b, j: (b, j, 0)
    const = lambda b, j: (0, 0)
    full = lambda a: pl.BlockSpec(a.shape, const)
    nw = heads * LANES
    outs = [(nw, BF16), (nw, BF16), (nw, BF16), (LANES, BF16), (kvl, F32), (cr, F32), (nw, BF16), (nw, F32)]
    return pl.pallas_call(
        functools.partial(_odd_in_kernel, dims),
        grid=(bn, ln // tm),
        in_specs=[pl.BlockSpec((None, tm, d), row), full(ng), full(w), full(qag), full(wuq), full(kvg),
                  full(wuk), full(wuv), full(gqn), full(gqr), full(gkn), full(gkr),
                  pl.BlockSpec((tm, LANES), lambda b, j: (j, 0)), pl.BlockSpec((tm, LANES), lambda b, j: (j, 0))],
        out_specs=[pl.BlockSpec((None, tm, n), row) for n, _ in outs],
        out_shape=[jax.ShapeDtypeStruct((bn, ln, n), dt) for n, dt in outs],
        compiler_params=_params("parallel", "parallel"),
        name="odd_in",
    )(x, ng, w, qag, wuq, kvg, wuk, wuv, gqn, gqr, gkn, gkr, cos_t, sin_t)


def _mla_tile(scale, heads, qn_ref, qr_ref, kn_ref, kr_ref, v_ref, m_scr, l_scr, acc_scr, off, first):
    tq = qn_ref.shape[0]
    hd = qn_ref.shape[1] // heads
    hs = [slice(h * hd, (h + 1) * hd) for h in range(heads)]
    kr = kr_ref[pl.ds(off, tq), :]
    ss = [_dot_nt(jnp.concatenate([qn_ref[:, sl], qr_ref[:, sl]], axis=1),
                  jnp.concatenate([kn_ref[pl.ds(off, tq), sl], kr], axis=1)) * scale for sl in hs]
    if first:
        t = lax.broadcasted_iota(jnp.int32, (tq, tq), 0)
        u = lax.broadcasted_iota(jnp.int32, (tq, tq), 1)
        ss = [jnp.where(u <= t, s, -jnp.inf) for s in ss]
    ps, alphas = [], []
    for h, s in enumerate(ss):
        mx = jnp.broadcast_to(jnp.max(s, axis=-1, keepdims=True), s.shape)
        if first:
            m_new = mx
        else:
            m_old = m_scr[h]
            m_new = jnp.maximum(m_old, mx)
            alphas.append(jnp.exp(m_old - m_new))
        p = jnp.exp(s - m_new)
        row = jnp.sum(p, axis=-1, keepdims=True)
        l_scr[h] = jnp.broadcast_to(row, s.shape) if first else alphas[h] * l_scr[h] + row
        m_scr[h] = m_new
        ps.append(p.astype(BF16))
    for h, sl in enumerate(hs):
        pv = _dot(ps[h], v_ref[pl.ds(off, tq), sl])
        acc_scr[:, sl] = pv if first else alphas[h] * acc_scr[:, sl] + pv


def _mla_kernel(scale, heads, qn_ref, qr_ref, kn_ref, kr_ref, v_ref, o_ref, m_scr, l_scr, acc_scr):
    i = pl.program_id(1)
    tq = qn_ref.shape[0]
    hd = qn_ref.shape[1] // heads
    tile = functools.partial(_mla_tile, scale, heads, qn_ref, qr_ref, kn_ref, kr_ref, v_ref, m_scr, l_scr, acc_scr)
    tile(pl.multiple_of(i * tq, tq), True)

    def body(jj, carry):
        tile(pl.multiple_of(jj * tq, tq), False)
        return carry

    lax.fori_loop(0, i, body, 0)
    for h in range(heads):
        sl = slice(h * hd, (h + 1) * hd)
        o_ref[:, sl] = acc_scr[:, sl] / l_scr[h]


def _mla_prompt(qn, qr, kn, krp, v, heads, scale):
    bn, lp, w = qn.shape
    qspec = pl.BlockSpec((None, LANES, w), lambda b, i: (b, i, 0))
    kspec = pl.BlockSpec((None, lp, w), lambda b, i: (b, 0, 0))
    return pl.pallas_call(
        functools.partial(_mla_kernel, scale, heads),
        grid=(bn, lp // LANES),
        in_specs=[qspec, qspec, kspec, pl.BlockSpec((None, lp, LANES), lambda b, i: (b, 0, 0)), kspec],
        out_specs=qspec,
        out_shape=jax.ShapeDtypeStruct((bn, lp, w), F32),
        scratch_shapes=[pltpu.VMEM((heads, LANES, LANES), F32), pltpu.VMEM((heads, LANES, LANES), F32),
                        pltpu.VMEM((LANES, w), F32)],
        compiler_params=_params("parallel", "arbitrary"),
        name="mla_prompt",
    )(qn, qr, kn, krp, v)


def _odd_out_kernel(x_ref, o_ref, g_ref, w_ref, y_ref):
    og = (o_ref[...] * _silu(g_ref[...])).astype(BF16)
    y_ref[...] = x_ref[...] + _dot(og, w_ref[...])


def _odd_out(x, o, g, w, tm):
    bn, ln, d = x.shape
    row = lambda b, j: (b, j, 0)
    wide = pl.BlockSpec((None, tm, d), row)
    return pl.pallas_call(
        _odd_out_kernel,
        grid=(bn, ln // tm),
        in_specs=[wide, pl.BlockSpec((None, tm, o.shape[-1]), row), pl.BlockSpec((None, tm, g.shape[-1]), row),
                  pl.BlockSpec(w.shape, lambda b, j: (0, 0))],
        out_specs=wide,
        out_shape=jax.ShapeDtypeStruct(x.shape, F32),
        compiler_params=_params("parallel", "parallel"),
        name="odd_out",
    )(x, o, g, w)


def _odd_out_dec_kernel(heads, x_ref, ctx_ref, g_ref, wuv_ref, w_ref, y_ref):
    kvl = ctx_ref.shape[-1] // heads
    vd = wuv_ref.shape[-1] // heads
    y = x_ref[...]
    for h in range(heads):
        o = _dot(ctx_ref[:, h * kvl:(h + 1) * kvl].astype(BF16), wuv_ref[:, h * vd:(h + 1) * vd])
        og = (o * _silu(g_ref[:, h * vd:(h + 1) * vd])).astype(BF16)
        y = y + _dot(og, w_ref[h * vd:(h + 1) * vd, :])
    y_ref[...] = y


def _odd_out_dec(x, ctx, g, wuv, w, heads):
    full = lambda a: pl.BlockSpec(a.shape, lambda: (0,) * a.ndim)
    return pl.pallas_call(
        functools.partial(_odd_out_dec_kernel, heads),
        in_specs=[full(x), full(ctx), full(g), full(wuv), full(w)],
        out_specs=full(x),
        out_shape=jax.ShapeDtypeStruct(x.shape, F32),
        compiler_params=pltpu.CompilerParams(vmem_limit_bytes=VMEM_LIMIT),
        name="odd_out_dec",
    )(x, ctx, g, wuv, w)


def _column(row):
    n = row.shape[1]
    r = lax.broadcasted_iota(jnp.int32, (n, n), 0)
    c = lax.broadcasted_iota(jnp.int32, (n, n), 1)
    return jnp.sum(jnp.where(r == c, jnp.broadcast_to(row, (n, n)), 0.0), axis=1, keepdims=True)


def _hgrn_dec_kernel(q_ref, lf_ref, k_ref, v_ref, g_ref, gain_ref, s0_ref, o_ref, s_ref):
    heads = s0_ref.shape[0]
    for h in range(heads):
        sl = slice(h * LANES, (h + 1) * LANES)
        q, lf, k, v = q_ref[:, sl], lf_ref[:, sl], k_ref[:, sl], v_ref[:, sl]
        s0 = s0_ref[h]
        decay = jnp.exp(lf)
        qe = jnp.broadcast_to(q * decay, (8, LANES)).astype(BF16)
        o = (_dot(qe, s0.astype(BF16))[0:1] + jnp.sum(q * k, axis=-1, keepdims=True) * v)
        s_ref[h] = _column(decay) * s0 + _column(k) * v
        o_ref[:, sl] = _rms(o, gain_ref[...]) * _silu(g_ref[:, sl])


def _hgrn_decode(q, lf, k, v, g, gain, state, e):
    db, _, w = q.shape
    heads, dk, dv = state.shape[2:]
    tok = pl.BlockSpec((None, 1, w), lambda b: (b, 0, 0))
    return pl.pallas_call(
        _hgrn_dec_kernel,
        grid=(db,),
        in_specs=[tok, tok, tok, tok, tok, pl.BlockSpec((1, dv), lambda b: (0, 0)),
                  pl.BlockSpec((None, None, heads, dk, dv), lambda b: (e, b, 0, 0, 0))],
        out_specs=[tok, pl.BlockSpec((None, heads, dk, dv), lambda b: (b, 0, 0, 0))],
        out_shape=[jax.ShapeDtypeStruct((db, 1, w), F32), jax.ShapeDtypeStruct((db, heads, dk, dv), F32)],
        compiler_params=_params("parallel"),
        name="hgrn_decode",
    )(q, lf, k, v, g, gain, state)


def _interleave_constants(heads, ps):
    c = np.arange(heads * ps)
    dei = (c[:, None] // heads == np.arange(ps)[None, :]).astype(np.float32)
    return jnp.asarray(dei, BF16), jnp.asarray(dei.T, BF16)


def _sb_dec_kernel(scale, heads, pg, pt_ref, q_ref, uo_ref, dei_ref, rei_ref, *refs):
    k_refs, v_refs = refs[:pg], refs[pg:2 * pg]
    o_ref, acc_scr, later_scr = refs[2 * pg:]
    g = pl.program_id(1)

    @pl.when(g == 0)
    def _():
        acc_scr[...] = jnp.zeros_like(acc_scr)
        later_scr[...] = jnp.zeros_like(later_scr)

    q8 = q_ref[...]
    n = k_refs[0].shape[0]
    ps = n // heads
    m = pg * 8
    own = (lax.broadcasted_iota(jnp.int32, (m, n), 1) % heads) == (lax.broadcasted_iota(jnp.int32, (m, n), 0) % 8)
    order = list(range(pg - 1, -1, -1))

    zm = jnp.where(own, jnp.concatenate([_dot_nt(q8, k_refs[r][...].astype(BF16)) for r in order], axis=0), 0.0)
    hi, lo = _split2(zm)
    zz = _dot(jnp.concatenate([hi, lo], axis=0), dei_ref[...])
    z = (zz[:m] + zz[m:]) * scale
    sp = _softplus(z)
    hi, lo = _split2(sp)
    res = _dot(jnp.concatenate([hi, lo], axis=0), uo_ref[...])
    res = res[:m] + res[m:]
    later = later_scr[...]
    laters = []
    for i in range(pg):
        laters.append(later)
        later = later + res[8 * i:8 * i + 8, ps:]
    later_scr[...] = later
    a = jnp.exp(z - sp - res[:, :ps] - jnp.concatenate(laters, axis=0))
    a_rows = jnp.where(own, _dot(a.astype(BF16), rei_ref[...]), 0.0)
    acc = acc_scr[...]
    for i, r in enumerate(order):
        acc = acc + _dot(a_rows[8 * i:8 * i + 8].astype(BF16), v_refs[r][...].astype(BF16))
    acc_scr[...] = acc

    @pl.when(g == pl.num_programs(1) - 1)
    def _():
        o_ref[...] = acc


def _sb_decode(q8, cache_k, cache_v, page_table, e, heads, pg):
    db, _, hd = q8.shape
    n_pages = page_table.shape[1]
    n = cache_k.shape[2]
    ps = n // heads
    ng = n_pages // pg
    uo = _suffix_constant()
    dei, rei = _interleave_constants(heads, ps)

    def page_spec(r):
        return pl.BlockSpec((None, None, n, hd), lambda b, g, pt: (e, pt[b, (ng - 1 - g) * pg + r], 0, 0))

    tok = pl.BlockSpec((None, 8, hd), lambda b, g, pt: (b, 0, 0))
    const = lambda a: pl.BlockSpec(a.shape, lambda b, g, pt: (0, 0))
    grid_spec = pltpu.PrefetchScalarGridSpec(
        num_scalar_prefetch=1,
        grid=(db, ng),
        in_specs=[tok, const(uo), const(dei), const(rei)] + [page_spec(r) for r in range(pg)] * 2,
        out_specs=tok,
        scratch_shapes=[pltpu.VMEM((8, hd), F32), pltpu.VMEM((8, ps), F32)],
    )
    return pl.pallas_call(
        functools.partial(_sb_dec_kernel, hd ** -0.5, heads, pg),
        grid_spec=grid_spec,
        out_shape=jax.ShapeDtypeStruct((db, 8, hd), F32),
        compiler_params=_params("parallel", "arbitrary"),
        name="sb_decode",
    )(page_table, q8, uo, dei, rei, *([cache_k] * pg), *([cache_v] * pg))


def _mla_dec_kernel(scale, heads, pg, cr, pt_ref, qn_ref, qr_ref, knn_ref, krn_ref, cn_ref, gkn_ref,
                    wuk_ref, wukt_ref, *refs):
    c_refs, kr_refs = refs[:pg], refs[pg:2 * pg]
    ctx_ref, lhs_scr, qr_scr, cbf_scr, m_scr, l_scr, acc_scr = refs[2 * pg:]
    g = pl.program_id(1)
    nk = wukt_ref.shape[0]
    nope = nk // heads
    kvl = wukt_ref.shape[1]

    @pl.when(g == 0)
    def _():
        lhs_scr[0:nk, :] = wukt_ref[...]
        qg = qn_ref[...].astype(F32)
        self_s = jnp.zeros((8, LANES), F32)
        rowid = lax.broadcasted_iota(jnp.int32, (8, LANES), 0)
        absorbed = jnp.zeros((8, kvl), F32)
        arow = lax.broadcasted_iota(jnp.int32, (8, kvl), 0)
        for h in range(heads):
            sl = slice(h * nope, (h + 1) * nope)
            qh = qg[:, sl]
            qt = jnp.broadcast_to(qh * gkn_ref[...], (8, nope)).astype(BF16)
            u = _dot_nt(qt, wuk_ref[:, sl])
            absorbed = jnp.where(arow == h, u, absorbed)
            qr_h = qr_ref[:, h * LANES:(h + 1) * LANES].astype(F32)
            qr_scr[h:h + 1, :] = qr_h
            s_h = (jnp.sum(qh * knn_ref[:, sl].astype(F32), axis=-1, keepdims=True)
                   + jnp.sum(qr_h * krn_ref[...].astype(F32), axis=-1, keepdims=True)) * scale
            self_s = jnp.where(rowid == h, jnp.broadcast_to(s_h, (8, LANES)), self_s)
        lhs_scr[nk:nk + 16, :] = jnp.concatenate([absorbed, jnp.zeros_like(absorbed)], axis=0).astype(BF16)
        m_scr[...] = self_s
        l_scr[...] = jnp.ones_like(l_scr)
        acc_scr[...] = jnp.broadcast_to(cn_ref[...], acc_scr.shape)

    ps = c_refs[0].shape[0]
    for r in range(pg):
        cbf_scr[r * ps:(r + 1) * ps, :] = c_refs[r][...].astype(BF16)
    qr8 = qr_scr[...][:, :cr].astype(BF16)
    lhs = lhs_scr[...]
    width = 2 * ps
    kts = [_dot_nt(lhs, cbf_scr[t * width:(t + 1) * width, :]) for t in range(pg * ps // width)]
    raw = []
    for kt in kts:
        ssq = jnp.sum((kt[0:nk] * kt[0:nk]).reshape(heads, nope, width), axis=1)
        raw.append(kt[nk:nk + 8] * lax.rsqrt(ssq / nope + EPS))
    krt = jnp.concatenate([kr_refs[r][...] for r in range(pg)], axis=1).astype(BF16)
    s = (jnp.concatenate(raw, axis=1) + _dot(qr8, krt)) * scale
    m, l = m_scr[...], l_scr[...]
    m_new = jnp.maximum(m, jnp.max(s, axis=-1, keepdims=True))
    alpha = jnp.exp(m - m_new)
    p = jnp.exp(s - m_new[:, 0:1])
    l = alpha * l + jnp.sum(p, axis=-1, keepdims=True)
    acc = alpha[:, 0:1] * acc_scr[...] + _dot(p.astype(BF16), cbf_scr[...])
    m_scr[...], l_scr[...], acc_scr[...] = m_new, l, acc

    @pl.when(g == pl.num_programs(1) - 1)
    def _():
        ctx_ref[...] = acc / l[:, 0:1]


def _mla_decode(qn, qr, knn, krn, cn, gkn, wuk, wukt, cache_c, cache_krt, page_table, o, heads, pg, scale):
    db = qn.shape[0]
    n_pages = page_table.shape[1]
    ps, kvl = cache_c.shape[2:]
    cr = cache_krt.shape[2]
    ng = n_pages // pg
    nk = wukt.shape[0]

    def page_spec(r, rows, cols):
        return pl.BlockSpec((None, None, rows, cols), lambda b, g, pt: (o, pt[b, g * pg + r], 0, 0))

    tok = lambda a: pl.BlockSpec((None, 1, a.shape[-1]), lambda b, g, pt: (b, 0, 0))
    const = lambda a: pl.BlockSpec(a.shape, lambda b, g, pt: (0, 0))
    grid_spec = pltpu.PrefetchScalarGridSpec(
        num_scalar_prefetch=1,
        grid=(db, ng),
        in_specs=[tok(qn), tok(qr), tok(knn), tok(krn), tok(cn), const(gkn), const(wuk), const(wukt)]
                 + [page_spec(r, ps, kvl) for r in range(pg)] + [page_spec(r, cr, ps) for r in range(pg)],
        out_specs=pl.BlockSpec((None, 8, kvl), lambda b, g, pt: (b, 0, 0)),
        scratch_shapes=[pltpu.VMEM((nk + 16, kvl), BF16), pltpu.VMEM((8, LANES), F32),
                        pltpu.VMEM((pg * ps, kvl), BF16), pltpu.VMEM((8, LANES), F32),
                        pltpu.VMEM((8, LANES), F32), pltpu.VMEM((8, kvl), F32)],
    )
    return pl.pallas_call(
        functools.partial(_mla_dec_kernel, scale, heads, pg, cr),
        grid_spec=grid_spec,
        out_shape=jax.ShapeDtypeStruct((db, 8, kvl), F32),
        compiler_params=_params("parallel", "arbitrary"),
        name="mla_decode",
    )(page_table, qn, qr, knn, krn, cn, gkn, wuk, wukt, *([cache_c] * pg), *([cache_krt] * pg))


def _rope_tables(pos, cr):
    half = cr // 2
    inv = ROPE_THETA ** (-jnp.arange(half, dtype=F32) / half)
    ang = pos.astype(F32)[:, None] * inv[None, :]
    cos, sin = jnp.cos(ang), jnp.sin(ang)
    pad = LANES - cr
    n = pos.shape[0]
    return (jnp.concatenate([cos, cos, jnp.ones((n, pad), F32)], axis=1),
            jnp.concatenate([-sin, sin, jnp.zeros((n, pad), F32)], axis=1))


def _pad_lanes(a, width):
    return jnp.pad(a, [(0, 0)] * (a.ndim - 1) + [(0, width - a.shape[-1])])


def kernel(x_prompt, x_sample, state_hgrn, cache_sb_k, cache_sb_v, cache_mla_ckv, cache_mla_krope, page_table,
           meta_tokens, norm_gain, lb_logits, w_in_even, hgrn_out_gain, w_out_even, w_in_odd, q_a_gain, w_uq,
           kv_a_gain, w_uk, w_uv, qk_gain_q_nope, qk_gain_q_rope, qk_gain_k_nope, qk_gain_k_rope, w_out_odd):
    bp, seq, d = x_prompt.shape
    db, dec_t, _ = x_sample.shape
    assert dec_t == 1, "the sample-group kernels handle exactly one new token per sequence"
    n_meta = meta_tokens.shape[0]
    n_tok = seq + n_meta
    lp = -(-n_tok // LANES) * LANES
    depth = norm_gain.shape[0]
    a_heads, a_kd, a_vd = state_hgrn.shape[2:]
    n_phys, ps, b_heads, b_hd = cache_sb_k.shape[1:]
    assert a_kd == LANES and a_vd == LANES and b_hd == LANES and ps == LANES
    n_pages = page_table.shape[1]
    past = n_pages * ps
    ql, kvl = q_a_gain.shape[1], kv_a_gain.shape[1]
    c_nope, c_rope = qk_gain_q_nope.shape[1], qk_gain_q_rope.shape[1]
    c_heads = w_uk.shape[2] // c_nope
    assert c_nope == LANES and w_uv.shape[2] == c_heads * LANES and c_heads == 8 and c_rope <= LANES
    assert ql % LANES == 0 and kvl % LANES == 0
    mla_scale = (c_nope + c_rope) ** -0.5
    dims = (ql, kvl, c_rope, c_heads)
    tm = lp // 4 if (lp // 4) % 16 == 0 else LANES
    pg = 8 if n_pages % 8 == 0 else 2
    assert n_pages % pg == 0

    w_in_e = w_in_even.astype(BF16)
    w_out_e = w_out_even.astype(BF16)
    w_in_o = jnp.concatenate([w_in_odd[..., :ql + kvl], _pad_lanes(w_in_odd[..., ql + kvl:ql + kvl + c_rope], LANES),
                              w_in_odd[..., ql + kvl + c_rope:]], axis=-1).astype(BF16)
    wq = w_uq.reshape(w_uq.shape[0], ql, c_heads, c_nope + c_rope)
    w_uq_p = jnp.concatenate([wq[..., :c_nope].reshape(-1, ql, c_heads * c_nope),
                              _pad_lanes(wq[..., c_nope:], LANES).reshape(-1, ql, c_heads * LANES)],
                             axis=-1).astype(BF16)
    w_uk_b = w_uk.astype(BF16)
    w_uk_t = jnp.swapaxes(w_uk, 1, 2).astype(BF16)
    w_uv_b = w_uv.astype(BF16)
    w_out_o = w_out_odd.astype(BF16)
    g_qr = _pad_lanes(qk_gain_q_rope, LANES)
    g_kr = _pad_lanes(qk_gain_k_rope, LANES)

    def rowof(a, i):
        return a[i:i + 1]

    meta = jnp.broadcast_to(meta_tokens[None].astype(x_prompt.dtype), (bp, n_meta, d))
    x = jnp.concatenate([meta, x_prompt, jnp.zeros((bp, lp - n_tok, d), x_prompt.dtype)], axis=1)
    cos_p, sin_p = _rope_tables(jnp.arange(lp, dtype=jnp.int32), c_rope)
    p_s, p_k, p_v, p_c, p_kr = [], [], [], [], []
    for layer in range(depth):
        i = layer // 2
        ng = rowof(norm_gain, layer)
        if layer % 2 == 0:
            qa, lf, ka, ia, ga, qb, kb, vb, kb16, vb16, gb = _even_in(x, ng, lb_logits, w_in_e[i], layer, tm)
            oa, s_new = _hgrn_prompt(qa, lf, ka, ia, ga, rowof(hgrn_out_gain, i), n_tok, a_heads)
            ob = _sb_prompt(qb, kb16, vb16, b_heads)
            x = _even_out(x, oa, ob, gb, w_out_e[i], tm)
            p_s.append(s_new)
            p_k.append(kb[:, :n_tok].reshape(bp, n_tok, b_heads, b_hd))
            p_v.append(vb[:, :n_tok].reshape(bp, n_tok, b_heads, b_hd))
        else:
            qn, qr, kn, krp, c, kr, v, g = _odd_in(
                x, ng, w_in_o[i], rowof(q_a_gain, i), w_uq_p[i], rowof(kv_a_gain, i), w_uk_b[i], w_uv_b[i],
                rowof(qk_gain_q_nope, i), rowof(g_qr, i), rowof(qk_gain_k_nope, i), rowof(g_kr, i),
                cos_p, sin_p, dims, tm)
            o = _mla_prompt(qn, qr, kn, krp, v, c_heads, mla_scale)
            x = _odd_out(x, o, g, w_out_o[i], tm)
            p_c.append(c[:, :n_tok])
            p_kr.append(kr[:, :n_tok])
    y_prompt = x[:, n_meta:n_tok]

    xs = x_sample.reshape(1, db, d)
    cos_s, sin_s = _rope_tables(jnp.full((db,), past, jnp.int32), c_rope)
    sb_k = cache_sb_k.reshape(cache_sb_k.shape[0], n_phys, ps * b_heads, b_hd)
    sb_v = cache_sb_v.reshape(cache_sb_v.shape[0], n_phys, ps * b_heads, b_hd)
    mla_krt = jnp.swapaxes(cache_mla_krope, 2, 3)
    tokens = lambda a: a.reshape(db, 1, a.shape[-1])
    s_s, s_k, s_v, s_c, s_kr = [], [], [], [], []
    for layer in range(depth):
        i = layer // 2
        ng = rowof(norm_gain, layer)
        if layer % 2 == 0:
            qa, lf, ka, ia, ga, qb, kb, vb, _, _, gb = _even_in(xs, ng, lb_logits, w_in_e[i], layer, db)
            oa, s_new = _hgrn_decode(tokens(qa), tokens(lf), tokens(ka), tokens(ia), tokens(ga),
                                     rowof(hgrn_out_gain, i), state_hgrn, i)
            q8 = jnp.pad(qb.reshape(db, b_heads, b_hd), ((0, 0), (0, 8 - b_heads), (0, 0)))
            ob = _sb_decode(q8, sb_k, sb_v, page_table, i, b_heads, pg)[:, :b_heads]
            xs = _even_out(xs, oa.reshape(1, db, -1), ob.reshape(1, db, -1), gb, w_out_e[i], db)
            s_s.append(s_new)
            s_k.append(kb.reshape(db, 1, b_heads, b_hd))
            s_v.append(vb.reshape(db, 1, b_heads, b_hd))
        else:
            qn, qr, kn, krp, c, kr, _, g = _odd_in(
                xs, ng, w_in_o[i], rowof(q_a_gain, i), w_uq_p[i], rowof(kv_a_gain, i), w_uk_b[i], w_uv_b[i],
                rowof(qk_gain_q_nope, i), rowof(g_qr, i), rowof(qk_gain_k_nope, i), rowof(g_kr, i),
                cos_s, sin_s, dims, db)
            ctx = _mla_decode(tokens(qn), tokens(qr), tokens(kn), tokens(krp), tokens(c),
                              rowof(qk_gain_k_nope, i), w_uk_b[i], w_uk_t[i], cache_mla_ckv, mla_krt,
                              page_table, i, c_heads, pg, mla_scale)
            ctx = ctx[:, :c_heads].reshape(db, c_heads * kvl)
            xs = _odd_out_dec(xs[0], ctx, g[0], w_uv_b[i], w_out_o[i], c_heads)[None]
            s_c.append(c.reshape(db, 1, kvl))
            s_kr.append(kr.reshape(db, 1, c_rope))
    y_sample = xs.reshape(db, 1, d)

    return (y_prompt, y_sample, jnp.stack(p_s), jnp.stack(p_k), jnp.stack(p_v), jnp.stack(p_c), jnp.stack(p_kr),
            jnp.stack(s_s), jnp.stack(s_k), jnp.stack(s_v), jnp.stack(s_c), jnp.stack(s_kr))
```

```python
import functools

import numpy as np
import jax
import jax.numpy as jnp
from jax import lax
from jax.experimental import pallas as pl
from jax.experimental.pallas import tpu as pltpu

EPS = 1e-6
ROPE_THETA = 10000.0
LANES = 128
VMEM_LIMIT = 56 * 1024 * 1024
F32 = jnp.float32
BF16 = jnp.bfloat16
GLA_LEVELS = (64, 32, 16, 8, 4, 2, 1)

NT = (((1,), (1,)), ((), ()))
TN = (((0,), (0,)), ((), ()))


def _params(*sem):
    return pltpu.CompilerParams(dimension_semantics=sem, vmem_limit_bytes=VMEM_LIMIT)


def _dot(a, b):
    return jnp.dot(a, b, preferred_element_type=F32)


def _dot_nt(a, b):
    return lax.dot_general(a, b, NT, preferred_element_type=F32)


def _rms(x, gain, n=None):
    ss = jnp.sum(x * x, axis=-1, keepdims=True)
    ms = ss / (x.shape[-1] if n is None else n)
    return x * lax.rsqrt(ms + EPS) * gain


def _sigmoid(x):
    return 1.0 / (1.0 + jnp.exp(-x))


def _silu(x):
    return x * _sigmoid(x)


def _softplus(z):
    return jnp.maximum(z, 0.0) + jnp.log(1.0 + jnp.exp(-jnp.abs(z)))


def _split2_lanes(x):
    return jnp.concatenate(_split2(x), axis=1)


def _split2(x):
    hi = x.astype(BF16)
    lo = (x - hi.astype(F32)).astype(BF16)
    return hi, lo


def _split3(x):
    hi = x.astype(BF16)
    r = x - hi.astype(F32)
    mid = r.astype(BF16)
    lo = (r - mid.astype(F32)).astype(BF16)
    return hi, mid, lo


def _rope(x, cos_t, sin_t, half):
    lane = lax.broadcasted_iota(jnp.int32, x.shape, 1)
    swapped = jnp.where(lane < half, pltpu.roll(x, LANES - half, 1), pltpu.roll(x, half, 1))
    return x * cos_t + swapped * sin_t


def _even_in_kernel(layer, qb_scale, x_ref, ng_ref, lbl_ref, w_ref,
                    qa_ref, lf_ref, ka_ref, ia_ref, ga_ref, qb_ref, kb_ref, vb_ref, kb16_ref, vb16_ref, gb_ref):
    gw = qa_ref.shape[-1]
    hb = _rms(x_ref[...], ng_ref[...]).astype(BF16)

    def proj(g):
        return _dot(hb, w_ref[:, g * gw:(g + 1) * gw])

    logits = lbl_ref[...]
    e = jnp.exp(logits - jnp.max(logits, axis=0, keepdims=True))
    sm = e / jnp.sum(e, axis=0, keepdims=True)
    cum = sm[0:1]
    for i in range(1, layer + 1):
        cum = cum + sm[i:i + 1]
    lb = cum - sm[0:1]

    qa_ref[...] = proj(0)
    f = lb + (1.0 - lb) * _sigmoid(proj(1))
    lf_ref[...] = jnp.log(f)
    ka_ref[...] = 1.0 - f
    ia_ref[...] = proj(2)
    ga_ref[...] = proj(3)
    qb_ref[...] = (proj(4) * qb_scale).astype(BF16)
    kb = proj(5)
    kb_ref[...] = kb
    kb16_ref[...] = kb.astype(BF16)
    vb = proj(6)
    vb_ref[...] = vb
    vb16_ref[...] = vb.astype(BF16)
    gb_ref[...] = proj(7)


def _even_in(x, ng, lbl, w, layer, qb_scale, tm):
    bn, ln, d = x.shape
    gw = w.shape[1] // 8
    row = lambda b, j: (b, j, 0)
    const = lambda b, j: (0, 0)
    out_dt = [F32, F32, F32, F32, F32, BF16, F32, F32, BF16, BF16, F32]
    return pl.pallas_call(
        functools.partial(_even_in_kernel, layer, qb_scale),
        grid=(bn, ln // tm),
        in_specs=[pl.BlockSpec((None, tm, d), row), pl.BlockSpec((1, d), const),
                  pl.BlockSpec(lbl.shape, const), pl.BlockSpec(w.shape, const)],
        out_specs=[pl.BlockSpec((None, tm, gw), row)] * len(out_dt),
        out_shape=[jax.ShapeDtypeStruct((bn, ln, gw), dt) for dt in out_dt],
        compiler_params=_params("parallel", "parallel"),
        name="even_in",
    )(x, ng, lbl, w)


def _gla_constants():
    c = LANES
    t = np.arange(c)[:, None]
    s = np.arange(c)[None, :]
    blocks = [(s <= t)]
    sel, mask = [], []
    for hs in GLA_LEVELS:
        grp = t // (2 * hs)
        mid = grp * 2 * hs + hs - 1
        upper = (t % (2 * hs)) >= hs
        w = np.where(upper, (s > mid) & (s <= t), (s > t) & (s <= mid))
        blocks.append(w)
        sel.append(np.broadcast_to(upper, (c, c)))
        mask.append(upper & ((s % (2 * hs)) < hs) & ((s // (2 * hs)) == grp))
    wall = np.concatenate(blocks, axis=0).astype(np.float32)
    wall = np.concatenate([wall, wall, wall], axis=1)
    return (jnp.asarray(wall, BF16), jnp.asarray(np.stack(sel).astype(np.float32)),
            jnp.asarray(np.stack(mask).astype(np.float32)))


def _hgrn_kernel(n_valid, wall_ref, sel_ref, mask_ref, q_ref, lf_ref, k_ref, v_ref, g_ref, gain_ref,
                 o_ref, s_ref, st_scr):
    c = pl.program_id(1)
    heads = st_scr.shape[0]
    tc = q_ref.shape[0]

    @pl.when(c == 0)
    def _():
        st_scr[...] = jnp.zeros_like(st_scr)

    row = c * tc + lax.broadcasted_iota(jnp.int32, (tc, 1), 0)
    valid = row < n_valid
    lf_all = jnp.where(valid, lf_ref[...], 0.0)
    sums = _dot(wall_ref[...], jnp.concatenate(_split3(lf_all), axis=0))

    hs = [slice(h * LANES, (h + 1) * LANES) for h in range(heads)]
    qs = [q_ref[:, sl] for sl in hs]
    ks = [jnp.where(valid, k_ref[:, sl], 0.0) for sl in hs]
    vbs = [v_ref[:, sl].astype(BF16) for sl in hs]
    bs = [sums[0:tc, sl] for sl in hs]
    xs = [[(jnp.where(sel_ref[li] > 0.5, qs[h], ks[h]) * jnp.exp(sums[(li + 1) * tc:(li + 2) * tc, hs[h]])).astype(BF16)
           for li in range(len(GLA_LEVELS))] for h in range(heads)]
    grams = [[_dot_nt(x, x) for x in xh] for xh in xs]
    atts = []
    for gh in grams:
        att = gh[0] * mask_ref[0]
        for li in range(1, len(GLA_LEVELS)):
            att = att + gh[li] * mask_ref[li]
        atts.append(att.astype(BF16))
    sts = [st_scr[h] for h in range(heads)]
    inter = [_dot_nt((qs[h] * jnp.exp(bs[h])).astype(BF16), sts[h].astype(BF16)) for h in range(heads)]
    intra = [_dot(atts[h], vbs[h]) for h in range(heads)]
    for h, sl in enumerate(hs):
        b_last = bs[h][tc - 1:tc, :]
        kt = (ks[h] * jnp.exp(b_last - bs[h])).astype(BF16)
        st_scr[h] = sts[h] * jnp.exp(b_last) + lax.dot_general(vbs[h], kt, TN, preferred_element_type=F32)
        o = inter[h] + intra[h] + jnp.sum(qs[h] * ks[h], axis=-1, keepdims=True) * v_ref[:, sl]
        o_ref[:, sl] = _rms(o, gain_ref[...]) * _silu(g_ref[:, sl])

    @pl.when(c == pl.num_programs(1) - 1)
    def _():
        for h in range(heads):
            s_ref[h] = st_scr[h].T


def _hgrn_prompt(q, lf, k, v, g, gain, n_valid, heads):
    bn, lp, w = q.shape
    wall, sel, mask = _gla_constants()
    tok = pl.BlockSpec((None, LANES, w), lambda b, c: (b, c, 0))
    c2 = lambda b, c: (0, 0)
    c3 = lambda b, c: (0, 0, 0)
    dk = w // heads
    return pl.pallas_call(
        functools.partial(_hgrn_kernel, n_valid),
        grid=(bn, lp // LANES),
        in_specs=[pl.BlockSpec(wall.shape, c2), pl.BlockSpec(sel.shape, c3), pl.BlockSpec(mask.shape, c3),
                  tok, tok, tok, tok, tok, pl.BlockSpec((1, dk), c2)],
        out_specs=[tok, pl.BlockSpec((None, heads, dk, dk), lambda b, c: (b, 0, 0, 0))],
        out_shape=[jax.ShapeDtypeStruct((bn, lp, w), F32), jax.ShapeDtypeStruct((bn, heads, dk, dk), F32)],
        scratch_shapes=[pltpu.VMEM((heads, dk, dk), F32)],
        compiler_params=_params("parallel", "arbitrary"),
        name="hgrn_prompt",
    )(wall, sel, mask, q, lf, k, v, g, gain)


def _suffix_constant():
    s = np.arange(LANES)
    tri = (s[:, None] > s[None, :]).astype(np.float32)
    uo = np.concatenate([tri, np.ones((LANES, LANES), np.float32)], axis=1)
    return jnp.asarray(np.concatenate([uo, uo], axis=0), BF16)


def _sb_tile(heads, q_ref, k_ref, v_ref, uo, later_scr, acc_scr, off, nsub, first):
    tq = q_ref.shape[0]
    width = nsub * tq
    hd = q_ref.shape[1] // heads
    hs = [slice(h * hd, (h + 1) * hd) for h in range(heads)]
    subs = [slice(j * tq, (j + 1) * tq) for j in range(nsub)]
    zs = [_dot_nt(q_ref[:, sl], k_ref[pl.ds(off, width), sl]) for sl in hs]
    sps = [_softplus(z) for z in zs]
    if first:
        t = lax.broadcasted_iota(jnp.int32, (tq, width), 0)
        s = lax.broadcasted_iota(jnp.int32, (tq, width), 1)
        vis = s < t
        sps = [jnp.where(vis, sp, 0.0) for sp in sps]
    res = _dot(jnp.concatenate([_split2_lanes(sp[:, sub]) for sp in sps for sub in subs], axis=0), uo)
    for h, sl in enumerate(hs):
        later = None if first else later_scr[h]
        parts = [None] * nsub
        for j in range(nsub - 1, -1, -1):
            r = res[(h * nsub + j) * tq:(h * nsub + j + 1) * tq]
            log_a = zs[h][:, subs[j]] - sps[h][:, subs[j]] - r[:, :tq]
            parts[j] = jnp.exp(log_a if later is None else log_a - later)
            later = r[:, tq:] if later is None else later + r[:, tq:]
        later_scr[h] = later
        a = parts[0] if nsub == 1 else jnp.concatenate(parts, axis=1)
        if first:
            a = jnp.where(vis, a, 0.0)
        av = _dot(a.astype(BF16), v_ref[pl.ds(off, width), sl])
        acc_scr[:, sl] = av if first else acc_scr[:, sl] + av


def _sb_kernel(heads, q_ref, k_ref, v_ref, uo_ref, o_ref, later_scr, acc_scr):
    i = pl.program_id(1)
    tq = q_ref.shape[0]
    uo = uo_ref[...]
    tile = functools.partial(_sb_tile, heads, q_ref, k_ref, v_ref, uo, later_scr, acc_scr)
    tile(pl.multiple_of(i * tq, tq), 1, True)
    pairs = i // 2

    @pl.when(i % 2 == 1)
    def _():
        tile(pl.multiple_of((i - 1) * tq, tq), 1, False)

    def body(jj, carry):
        tile(pl.multiple_of((pairs - 1 - jj) * 2 * tq, 2 * tq), 2, False)
        return carry

    lax.fori_loop(0, pairs, body, 0)
    o_ref[...] = acc_scr[...]


def _sb_prompt(q, k, v, heads):
    bn, lp, w = q.shape
    uo = _suffix_constant()
    qspec = pl.BlockSpec((None, LANES, w), lambda b, i: (b, i, 0))
    kspec = pl.BlockSpec((None, lp, w), lambda b, i: (b, 0, 0))
    return pl.pallas_call(
        functools.partial(_sb_kernel, heads),
        grid=(bn, lp // LANES),
        in_specs=[qspec, kspec, kspec, pl.BlockSpec(uo.shape, lambda b, i: (0, 0))],
        out_specs=qspec,
        out_shape=jax.ShapeDtypeStruct((bn, lp, w), F32),
        scratch_shapes=[pltpu.VMEM((heads, LANES, LANES), F32), pltpu.VMEM((LANES, w), F32)],
        compiler_params=_params("parallel", "arbitrary"),
        name="sb_prompt",
    )(q, k, v, uo)


def _even_out_kernel(x_ref, oa_ref, ob_ref, gb_ref, w_ref, y_ref):
    wa = oa_ref.shape[-1]
    ob = ob_ref[...] * _silu(gb_ref[...])
    y_ref[...] = (x_ref[...] + _dot(oa_ref[...].astype(BF16), w_ref[0:wa, :])
                  + _dot(ob.astype(BF16), w_ref[wa:, :]))


def _even_out(x, oa, ob, gb, w, tm):
    bn, ln, d = x.shape
    row = lambda b, j: (b, j, 0)
    wide = pl.BlockSpec((None, tm, d), row)
    half = pl.BlockSpec((None, tm, oa.shape[-1]), row)
    return pl.pallas_call(
        _even_out_kernel,
        grid=(bn, ln // tm),
        in_specs=[wide, half, half, half, pl.BlockSpec(w.shape, lambda b, j: (0, 0))],
        out_specs=wide,
        out_shape=jax.ShapeDtypeStruct(x.shape, F32),
        compiler_params=_params("parallel", "parallel"),
        name="even_out",
    )(x, oa, ob, gb, w)


def _odd_in_kernel(dims, x_ref, ng_ref, w_ref, qag_ref, wuq_ref, kvg_ref, wuk_ref,
                   gqn_ref, gqr_ref, gkn_ref, gkr_ref, cos_ref, sin_ref,
                   qn_ref, qr_ref, kn_ref, krp_ref, c_ref, kr_ref, g_ref):
    ql, kvl, cr, heads = dims
    scale = (LANES + cr) ** -0.5
    hb = _rms(x_ref[...], ng_ref[...]).astype(BF16)
    y = _dot(hb, w_ref[...])
    o1, o2 = ql + kvl, ql + kvl + LANES
    cos_t, sin_t = cos_ref[...], sin_ref[...]
    g_ref[...] = y[:, o2:]
    c = _rms(y[:, ql:o1], kvg_ref[...])
    c_ref[...] = c
    kr = _rope(_rms(y[:, o1:o2], gkr_ref[...], cr), cos_t, sin_t, cr // 2)
    kr_ref[...] = kr[:, :cr]
    krp_ref[...] = kr.astype(BF16)
    cb = c.astype(BF16)
    kn = _dot(cb, wuk_ref[...])
    q = _dot(_rms(y[:, :ql], qag_ref[...]).astype(BF16), wuq_ref[...])
    nw = heads * LANES
    for h in range(heads):
        sl = slice(h * LANES, (h + 1) * LANES)
        kn_ref[:, sl] = _rms(kn[:, sl], gkn_ref[...]).astype(BF16)
        qn_ref[:, sl] = (_rms(q[:, sl], gqn_ref[...]) * scale).astype(BF16)
        qr = _rms(q[:, nw + h * LANES:nw + (h + 1) * LANES], gqr_ref[...], cr)
        qr_ref[:, sl] = (_rope(qr, cos_t, sin_t, cr // 2) * scale).astype(BF16)


def _odd_in(x, ng, w, qag, wuq, kvg, wuk, gqn, gqr, gkn, gkr, cos_t, sin_t, dims, tm):
    bn, ln, d = x.shape
    ql, kvl, cr, heads = dims
    row = lambda b, j: (b, j, 0)
    const = lambda b, j: (0, 0)
    full = lambda a: pl.BlockSpec(a.shape, const)
    nw = heads * LANES
    outs = [(nw, BF16), (nw, BF16), (nw, BF16), (LANES, BF16), (kvl, F32), (cr, F32), (nw, F32)]
    return pl.pallas_call(
        functools.partial(_odd_in_kernel, dims),
        grid=(bn, ln // tm),
        in_specs=[pl.BlockSpec((None, tm, d), row), full(ng), full(w), full(qag), full(wuq), full(kvg),
                  full(wuk), full(gqn), full(gqr), full(gkn), full(gkr),
                  pl.BlockSpec((tm, LANES), lambda b, j: (j, 0)), pl.BlockSpec((tm, LANES), lambda b, j: (j, 0))],
        out_specs=[pl.BlockSpec((None, tm, n), row) for n, _ in outs],
        out_shape=[jax.ShapeDtypeStruct((bn, ln, n), dt) for n, dt in outs],
        compiler_params=_params("parallel", "parallel"),
        name="odd_in",
    )(x, ng, w, qag, wuq, kvg, wuk, gqn, gqr, gkn, gkr, cos_t, sin_t)


def _mla_vt_kernel(c_ref, wuvt_ref, vt_ref):
    vt_ref[...] = _dot_nt(wuvt_ref[...], c_ref[...].astype(BF16)).astype(BF16)


def _mla_values_t(c, wuvt):
    bn, lp, kvl = c.shape
    nv = wuvt.shape[0]
    return pl.pallas_call(
        _mla_vt_kernel,
        grid=(bn, lp // LANES),
        in_specs=[pl.BlockSpec((None, LANES, kvl), lambda b, j: (b, j, 0)), pl.BlockSpec(wuvt.shape, lambda b, j: (0, 0))],
        out_specs=pl.BlockSpec((None, nv, LANES), lambda b, j: (b, 0, j)),
        out_shape=jax.ShapeDtypeStruct((bn, nv, lp), BF16),
        compiler_params=_params("parallel", "parallel"),
        name="mla_values_t",
    )(c, wuvt)


def _mla_tile(heads, qn_ref, qr_ref, kn_ref, kr_ref, vt_ref, m_scr, l_scr, acc_scr, off, width, first):
    tq = qn_ref.shape[0]
    hd = qn_ref.shape[1] // heads
    hs = [slice(h * hd, (h + 1) * hd) for h in range(heads)]
    kr = kr_ref[pl.ds(off, width), :]
    ss = [_dot_nt(jnp.concatenate([kn_ref[pl.ds(off, width), sl], kr], axis=1),
                  jnp.concatenate([qn_ref[:, sl], qr_ref[:, sl]], axis=1)) for sl in hs]
    if first:
        u = lax.broadcasted_iota(jnp.int32, (width, tq), 0)
        t = lax.broadcasted_iota(jnp.int32, (width, tq), 1)
        ss = [jnp.where(u <= t, s, -jnp.inf) for s in ss]
    ps, alphas = [], []
    for h, s in enumerate(ss):
        mx = jnp.max(s, axis=0, keepdims=True)
        if first:
            m_new = mx
        else:
            m_old = m_scr[h:h + 1, :]
            m_new = jnp.maximum(m_old, mx)
            alphas.append(jnp.exp(m_old - m_new))
        p = jnp.exp(s - m_new)
        col = jnp.sum(p, axis=0, keepdims=True)
        l_scr[h:h + 1, :] = col if first else alphas[h] * l_scr[h:h + 1, :] + col
        m_scr[h:h + 1, :] = m_new
        ps.append(p.astype(BF16))
    for h, sl in enumerate(hs):
        pv = _dot(vt_ref[sl, pl.ds(off, width)], ps[h])
        acc_scr[sl, :] = pv if first else alphas[h] * acc_scr[sl, :] + pv


def _mla_kernel(heads, qn_ref, qr_ref, kn_ref, kr_ref, vt_ref, o_ref, m_scr, l_scr, acc_scr):
    i = pl.program_id(1)
    tq = qn_ref.shape[0]
    hd = qn_ref.shape[1] // heads
    tile = functools.partial(_mla_tile, heads, qn_ref, qr_ref, kn_ref, kr_ref, vt_ref, m_scr, l_scr, acc_scr)
    tile(pl.multiple_of(i * tq, tq), tq, True)

    @pl.when(i % 2 == 1)
    def _():
        tile(pl.multiple_of((i - 1) * tq, tq), tq, False)

    def body(jj, carry):
        tile(pl.multiple_of(jj * 2 * tq, 2 * tq), 2 * tq, False)
        return carry

    lax.fori_loop(0, i // 2, body, 0)
    for h in range(heads):
        sl = slice(h * hd, (h + 1) * hd)
        o_ref[:, sl] = (acc_scr[sl, :] / l_scr[h:h + 1, :]).T


def _mla_prompt(qn, qr, kn, krp, vt, heads):
    bn, lp, w = qn.shape
    qspec = pl.BlockSpec((None, LANES, w), lambda b, i: (b, i, 0))
    kspec = pl.BlockSpec((None, lp, w), lambda b, i: (b, 0, 0))
    return pl.pallas_call(
        functools.partial(_mla_kernel, heads),
        grid=(bn, lp // LANES),
        in_specs=[qspec, qspec, kspec, pl.BlockSpec((None, lp, LANES), lambda b, i: (b, 0, 0)),
                  pl.BlockSpec((None, w, lp), lambda b, i: (b, 0, 0))],
        out_specs=qspec,
        out_shape=jax.ShapeDtypeStruct((bn, lp, w), F32),
        scratch_shapes=[pltpu.VMEM((8, LANES), F32), pltpu.VMEM((8, LANES), F32), pltpu.VMEM((w, LANES), F32)],
        compiler_params=_params("parallel", "arbitrary"),
        name="mla_prompt",
    )(qn, qr, kn, krp, vt)


def _odd_out_kernel(x_ref, o_ref, g_ref, w_ref, y_ref):
    og = (o_ref[...] * _silu(g_ref[...])).astype(BF16)
    y_ref[...] = x_ref[...] + _dot(og, w_ref[...])


def _odd_out(x, o, g, w, tm):
    bn, ln, d = x.shape
    row = lambda b, j: (b, j, 0)
    wide = pl.BlockSpec((None, tm, d), row)
    return pl.pallas_call(
        _odd_out_kernel,
        grid=(bn, ln // tm),
        in_specs=[wide, pl.BlockSpec((None, tm, o.shape[-1]), row), pl.BlockSpec((None, tm, g.shape[-1]), row),
                  pl.BlockSpec(w.shape, lambda b, j: (0, 0))],
        out_specs=wide,
        out_shape=jax.ShapeDtypeStruct(x.shape, F32),
        compiler_params=_params("parallel", "parallel"),
        name="odd_out",
    )(x, o, g, w)


def _odd_out_dec_kernel(heads, x_ref, ctx_ref, g_ref, wuv_ref, w_ref, y_ref):
    kvl = ctx_ref.shape[-1] // heads
    vd = wuv_ref.shape[-1] // heads
    y = x_ref[...]
    for h in range(heads):
        o = _dot(ctx_ref[:, h * kvl:(h + 1) * kvl].astype(BF16), wuv_ref[:, h * vd:(h + 1) * vd])
        og = (o * _silu(g_ref[:, h * vd:(h + 1) * vd])).astype(BF16)
        y = y + _dot(og, w_ref[h * vd:(h + 1) * vd, :])
    y_ref[...] = y


def _odd_out_dec(x, ctx, g, wuv, w, heads):
    full = lambda a: pl.BlockSpec(a.shape, lambda: (0,) * a.ndim)
    return pl.pallas_call(
        functools.partial(_odd_out_dec_kernel, heads),
        in_specs=[full(x), full(ctx), full(g), full(wuv), full(w)],
        out_specs=full(x),
        out_shape=jax.ShapeDtypeStruct(x.shape, F32),
        compiler_params=pltpu.CompilerParams(vmem_limit_bytes=VMEM_LIMIT),
        name="odd_out_dec",
    )(x, ctx, g, wuv, w)


def _column(row):
    n = row.shape[1]
    r = lax.broadcasted_iota(jnp.int32, (n, n), 0)
    c = lax.broadcasted_iota(jnp.int32, (n, n), 1)
    return jnp.sum(jnp.where(r == c, jnp.broadcast_to(row, (n, n)), 0.0), axis=1, keepdims=True)


def _hgrn_dec_kernel(q_ref, lf_ref, k_ref, v_ref, g_ref, gain_ref, s0_ref, o_ref, s_ref):
    heads = s0_ref.shape[0]
    for h in range(heads):
        sl = slice(h * LANES, (h + 1) * LANES)
        q, lf, k, v = q_ref[:, sl], lf_ref[:, sl], k_ref[:, sl], v_ref[:, sl]
        s0 = s0_ref[h]
        decay = jnp.exp(lf)
        qe = jnp.broadcast_to(q * decay, (8, LANES)).astype(BF16)
        o = (_dot(qe, s0.astype(BF16))[0:1] + jnp.sum(q * k, axis=-1, keepdims=True) * v)
        s_ref[h] = _column(decay) * s0 + _column(k) * v
        o_ref[:, sl] = _rms(o, gain_ref[...]) * _silu(g_ref[:, sl])


def _hgrn_decode(q, lf, k, v, g, gain, state, e):
    db, _, w = q.shape
    heads, dk, dv = state.shape[2:]
    tok = pl.BlockSpec((None, 1, w), lambda b: (b, 0, 0))
    return pl.pallas_call(
        _hgrn_dec_kernel,
        grid=(db,),
        in_specs=[tok, tok, tok, tok, tok, pl.BlockSpec((1, dv), lambda b: (0, 0)),
                  pl.BlockSpec((None, None, heads, dk, dv), lambda b: (e, b, 0, 0, 0))],
        out_specs=[tok, pl.BlockSpec((None, heads, dk, dv), lambda b: (b, 0, 0, 0))],
        out_shape=[jax.ShapeDtypeStruct((db, 1, w), F32), jax.ShapeDtypeStruct((db, heads, dk, dv), F32)],
        compiler_params=_params("parallel"),
        name="hgrn_decode",
    )(q, lf, k, v, g, gain, state)


def _interleave_constants(heads, ps):
    c = np.arange(heads * ps)
    dei = (c[:, None] // heads == np.arange(ps)[None, :]).astype(np.float32)
    return jnp.asarray(np.concatenate([dei, dei], axis=0), BF16), jnp.asarray(dei.T, BF16)


def _sb_dec_kernel(heads, pg, pt_ref, q_ref, uo_ref, dei_ref, rei_ref, *refs):
    k_refs, v_refs = refs[:pg], refs[pg:2 * pg]
    o_ref, acc_scr, later_scr = refs[2 * pg:]
    g = pl.program_id(1)

    @pl.when(g == 0)
    def _():
        acc_scr[...] = jnp.zeros_like(acc_scr)
        later_scr[...] = jnp.zeros_like(later_scr)

    q8 = q_ref[...]
    n = k_refs[0].shape[0]
    ps = n // heads
    m = pg * 8
    own = (lax.broadcasted_iota(jnp.int32, (m, n), 1) % heads) == (lax.broadcasted_iota(jnp.int32, (m, n), 0) % 8)
    order = list(range(pg - 1, -1, -1))

    zm = jnp.where(own, jnp.concatenate([_dot_nt(q8, k_refs[r][...].astype(BF16)) for r in order], axis=0), 0.0)
    z = _dot(_split2_lanes(zm), dei_ref[...])
    sp = _softplus(z)
    res = _dot(_split2_lanes(sp), uo_ref[...])
    later = later_scr[...]
    laters = []
    for i in range(pg):
        laters.append(later)
        later = later + res[8 * i:8 * i + 8, ps:]
    later_scr[...] = later
    a = jnp.exp(z - sp - res[:, :ps] - jnp.concatenate(laters, axis=0))
    a_rows = jnp.where(own, _dot(a.astype(BF16), rei_ref[...]), 0.0)
    acc = acc_scr[...]
    for i, r in enumerate(order):
        acc = acc + _dot(a_rows[8 * i:8 * i + 8].astype(BF16), v_refs[r][...].astype(BF16))
    acc_scr[...] = acc

    @pl.when(g == pl.num_programs(1) - 1)
    def _():
        o_ref[...] = acc


def _sb_decode(q8, cache_k, cache_v, page_table, e, heads, pg):
    db, _, hd = q8.shape
    n_pages = page_table.shape[1]
    n = cache_k.shape[2]
    ps = n // heads
    ng = n_pages // pg
    uo = _suffix_constant()
    dei, rei = _interleave_constants(heads, ps)

    def page_spec(r):
        return pl.BlockSpec((None, None, n, hd), lambda b, g, pt: (e, pt[b, (ng - 1 - g) * pg + r], 0, 0))

    tok = pl.BlockSpec((None, 8, hd), lambda b, g, pt: (b, 0, 0))
    const = lambda a: pl.BlockSpec(a.shape, lambda b, g, pt: (0, 0))
    grid_spec = pltpu.PrefetchScalarGridSpec(
        num_scalar_prefetch=1,
        grid=(db, ng),
        in_specs=[tok, const(uo), const(dei), const(rei)] + [page_spec(r) for r in range(pg)] * 2,
        out_specs=tok,
        scratch_shapes=[pltpu.VMEM((8, hd), F32), pltpu.VMEM((8, ps), F32)],
    )
    return pl.pallas_call(
        functools.partial(_sb_dec_kernel, heads, pg),
        grid_spec=grid_spec,
        out_shape=jax.ShapeDtypeStruct((db, 8, hd), F32),
        compiler_params=_params("parallel", "arbitrary"),
        name="sb_decode",
    )(page_table, q8, uo, dei, rei, *([cache_k] * pg), *([cache_v] * pg))


def _mla_dec_kernel(heads, pg, cr, pt_ref, qn_ref, qr_ref, knn_ref, krn_ref, cn_ref, gkn_ref,
                    wuk_ref, wukt_ref, *refs):
    c_refs, kr_refs = refs[:pg], refs[pg:2 * pg]
    ctx_ref, lhs_scr, qr_scr, cbf_scr, m_scr, l_scr, acc_scr = refs[2 * pg:]
    g = pl.program_id(1)
    nk = wukt_ref.shape[0]
    nope = nk // heads
    kvl = wukt_ref.shape[1]

    @pl.when(g == 0)
    def _():
        lhs_scr[0:nk, :] = wukt_ref[...]
        qg = qn_ref[...].astype(F32)
        self_s = jnp.zeros((8, LANES), F32)
        rowid = lax.broadcasted_iota(jnp.int32, (8, LANES), 0)
        absorbed = jnp.zeros((8, kvl), F32)
        arow = lax.broadcasted_iota(jnp.int32, (8, kvl), 0)
        for h in range(heads):
            sl = slice(h * nope, (h + 1) * nope)
            qh = qg[:, sl]
            qt = jnp.broadcast_to(qh * gkn_ref[...], (8, nope)).astype(BF16)
            u = _dot_nt(qt, wuk_ref[:, sl])
            absorbed = jnp.where(arow == h, u, absorbed)
            qr_h = qr_ref[:, h * LANES:(h + 1) * LANES].astype(F32)
            qr_scr[h:h + 1, :] = qr_h
            s_h = (jnp.sum(qh * knn_ref[:, sl].astype(F32), axis=-1, keepdims=True)
                   + jnp.sum(qr_h * krn_ref[...].astype(F32), axis=-1, keepdims=True))
            self_s = jnp.where(rowid == h, jnp.broadcast_to(s_h, (8, LANES)), self_s)
        lhs_scr[nk:nk + 16, :] = jnp.concatenate([absorbed, jnp.zeros_like(absorbed)], axis=0).astype(BF16)
        m_scr[...] = self_s
        l_scr[...] = jnp.ones_like(l_scr)
        acc_scr[...] = jnp.broadcast_to(cn_ref[...], acc_scr.shape)

    ps = c_refs[0].shape[0]
    for r in range(pg):
        cbf_scr[r * ps:(r + 1) * ps, :] = c_refs[r][...].astype(BF16)
    qr8 = qr_scr[...][:, :cr].astype(BF16)
    lhs = lhs_scr[...]
    width = 2 * ps
    kts = [_dot_nt(lhs, cbf_scr[t * width:(t + 1) * width, :]) for t in range(pg * ps // width)]
    raw = []
    for kt in kts:
        ssq = jnp.sum((kt[0:nk] * kt[0:nk]).reshape(heads, nope, width), axis=1)
        raw.append(kt[nk:nk + 8] * lax.rsqrt(ssq / nope + EPS))
    krt = jnp.concatenate([kr_refs[r][...] for r in range(pg)], axis=1).astype(BF16)
    s = jnp.concatenate(raw, axis=1) + _dot(qr8, krt)
    m, l = m_scr[...], l_scr[...]
    m_new = jnp.maximum(m, jnp.max(s, axis=-1, keepdims=True))
    alpha = jnp.exp(m - m_new)
    p = jnp.exp(s - m_new[:, 0:1])
    l = alpha * l + jnp.sum(p, axis=-1, keepdims=True)
    acc = alpha[:, 0:1] * acc_scr[...] + _dot(p.astype(BF16), cbf_scr[...])
    m_scr[...], l_scr[...], acc_scr[...] = m_new, l, acc

    @pl.when(g == pl.num_programs(1) - 1)
    def _():
        ctx_ref[...] = acc / l[:, 0:1]


def _mla_decode(qn, qr, knn, krn, cn, gkn, wuk, wukt, cache_c, cache_krt, page_table, o, heads, pg):
    db = qn.shape[0]
    n_pages = page_table.shape[1]
    ps, kvl = cache_c.shape[2:]
    cr = cache_krt.shape[2]
    ng = n_pages // pg
    nk = wukt.shape[0]

    def page_spec(r, rows, cols):
        return pl.BlockSpec((None, None, rows, cols), lambda b, g, pt: (o, pt[b, g * pg + r], 0, 0))

    tok = lambda a: pl.BlockSpec((None, 1, a.shape[-1]), lambda b, g, pt: (b, 0, 0))
    const = lambda a: pl.BlockSpec(a.shape, lambda b, g, pt: (0, 0))
    grid_spec = pltpu.PrefetchScalarGridSpec(
        num_scalar_prefetch=1,
        grid=(db, ng),
        in_specs=[tok(qn), tok(qr), tok(knn), tok(krn), tok(cn), const(gkn), const(wuk), const(wukt)]
                 + [page_spec(r, ps, kvl) for r in range(pg)] + [page_spec(r, cr, ps) for r in range(pg)],
        out_specs=pl.BlockSpec((None, 8, kvl), lambda b, g, pt: (b, 0, 0)),
        scratch_shapes=[pltpu.VMEM((nk + 16, kvl), BF16), pltpu.VMEM((8, LANES), F32),
                        pltpu.VMEM((pg * ps, kvl), BF16), pltpu.VMEM((8, LANES), F32),
                        pltpu.VMEM((8, LANES), F32), pltpu.VMEM((8, kvl), F32)],
    )
    return pl.pallas_call(
        functools.partial(_mla_dec_kernel, heads, pg, cr),
        grid_spec=grid_spec,
        out_shape=jax.ShapeDtypeStruct((db, 8, kvl), F32),
        compiler_params=_params("parallel", "arbitrary"),
        name="mla_decode",
    )(page_table, qn, qr, knn, krn, cn, gkn, wuk, wukt, *([cache_c] * pg), *([cache_krt] * pg))


def _rope_tables(pos, cr):
    half = cr // 2
    inv = ROPE_THETA ** (-jnp.arange(half, dtype=F32) / half)
    ang = pos.astype(F32)[:, None] * inv[None, :]
    cos, sin = jnp.cos(ang), jnp.sin(ang)
    pad = LANES - cr
    n = pos.shape[0]
    return (jnp.concatenate([cos, cos, jnp.ones((n, pad), F32)], axis=1),
            jnp.concatenate([-sin, sin, jnp.zeros((n, pad), F32)], axis=1))


def _pad_lanes(a, width):
    return jnp.pad(a, [(0, 0)] * (a.ndim - 1) + [(0, width - a.shape[-1])])


def kernel(x_prompt, x_sample, state_hgrn, cache_sb_k, cache_sb_v, cache_mla_ckv, cache_mla_krope, page_table,
           meta_tokens, norm_gain, lb_logits, w_in_even, hgrn_out_gain, w_out_even, w_in_odd, q_a_gain, w_uq,
           kv_a_gain, w_uk, w_uv, qk_gain_q_nope, qk_gain_q_rope, qk_gain_k_nope, qk_gain_k_rope, w_out_odd):
    bp, seq, d = x_prompt.shape
    db, dec_t, _ = x_sample.shape
    assert dec_t == 1, "the sample-group kernels handle exactly one new token per sequence"
    n_meta = meta_tokens.shape[0]
    n_tok = seq + n_meta
    lp = -(-n_tok // LANES) * LANES
    depth = norm_gain.shape[0]
    a_heads, a_kd, a_vd = state_hgrn.shape[2:]
    n_phys, ps, b_heads, b_hd = cache_sb_k.shape[1:]
    assert a_kd == LANES and a_vd == LANES and b_hd == LANES and ps == LANES
    n_pages = page_table.shape[1]
    past = n_pages * ps
    ql, kvl = q_a_gain.shape[1], kv_a_gain.shape[1]
    c_nope, c_rope = qk_gain_q_nope.shape[1], qk_gain_q_rope.shape[1]
    c_heads = w_uk.shape[2] // c_nope
    assert c_nope == LANES and w_uv.shape[2] == c_heads * LANES and c_heads == 8 and c_rope <= LANES
    assert ql % LANES == 0 and kvl % LANES == 0
    sb_scale = b_hd ** -0.5
    dims = (ql, kvl, c_rope, c_heads)
    tm = lp // 4 if (lp // 4) % 16 == 0 else LANES
    pg = 16 if n_pages % 16 == 0 else 2
    assert n_pages % pg == 0

    w_in_e = w_in_even.astype(BF16)
    w_out_e = w_out_even.astype(BF16)
    w_in_o = jnp.concatenate([w_in_odd[..., :ql + kvl], _pad_lanes(w_in_odd[..., ql + kvl:ql + kvl + c_rope], LANES),
                              w_in_odd[..., ql + kvl + c_rope:]], axis=-1).astype(BF16)
    wq = w_uq.reshape(w_uq.shape[0], ql, c_heads, c_nope + c_rope)
    w_uq_p = jnp.concatenate([wq[..., :c_nope].reshape(-1, ql, c_heads * c_nope),
                              _pad_lanes(wq[..., c_nope:], LANES).reshape(-1, ql, c_heads * LANES)],
                             axis=-1).astype(BF16)
    w_uk_b = w_uk.astype(BF16)
    w_uk_t = jnp.swapaxes(w_uk, 1, 2).astype(BF16)
    w_uv_b = w_uv.astype(BF16)
    w_uv_t = jnp.swapaxes(w_uv, 1, 2).astype(BF16)
    w_out_o = w_out_odd.astype(BF16)
    g_qr = _pad_lanes(qk_gain_q_rope, LANES)
    g_kr = _pad_lanes(qk_gain_k_rope, LANES)

    def rowof(a, i):
        return a[i:i + 1]

    meta = jnp.broadcast_to(meta_tokens[None].astype(x_prompt.dtype), (bp, n_meta, d))
    x = jnp.concatenate([meta, x_prompt, jnp.zeros((bp, lp - n_tok, d), x_prompt.dtype)], axis=1)
    cos_p, sin_p = _rope_tables(jnp.arange(lp, dtype=jnp.int32), c_rope)
    p_s, p_k, p_v, p_c, p_kr = [], [], [], [], []
    for layer in range(depth):
        i = layer // 2
        ng = rowof(norm_gain, layer)
        if layer % 2 == 0:
            qa, lf, ka, ia, ga, qb, kb, vb, kb16, vb16, gb = _even_in(x, ng, lb_logits, w_in_e[i], layer, sb_scale, tm)
            oa, s_new = _hgrn_prompt(qa, lf, ka, ia, ga, rowof(hgrn_out_gain, i), n_tok, a_heads)
            ob = _sb_prompt(qb, kb16, vb16, b_heads)
            x = _even_out(x, oa, ob, gb, w_out_e[i], tm)
            p_s.append(s_new)
            p_k.append(kb[:, :n_tok].reshape(bp, n_tok, b_heads, b_hd))
            p_v.append(vb[:, :n_tok].reshape(bp, n_tok, b_heads, b_hd))
        else:
            qn, qr, kn, krp, c, kr, g = _odd_in(
                x, ng, w_in_o[i], rowof(q_a_gain, i), w_uq_p[i], rowof(kv_a_gain, i), w_uk_b[i],
                rowof(qk_gain_q_nope, i), rowof(g_qr, i), rowof(qk_gain_k_nope, i), rowof(g_kr, i),
                cos_p, sin_p, dims, tm)
            o = _mla_prompt(qn, qr, kn, krp, _mla_values_t(c, w_uv_t[i]), c_heads)
            x = _odd_out(x, o, g, w_out_o[i], tm)
            p_c.append(c[:, :n_tok])
            p_kr.append(kr[:, :n_tok])
    y_prompt = x[:, n_meta:n_tok]

    xs = x_sample.reshape(1, db, d)
    cos_s, sin_s = _rope_tables(jnp.full((db,), past, jnp.int32), c_rope)
    sb_k = cache_sb_k.reshape(cache_sb_k.shape[0], n_phys, ps * b_heads, b_hd)
    sb_v = cache_sb_v.reshape(cache_sb_v.shape[0], n_phys, ps * b_heads, b_hd)
    mla_krt = jnp.swapaxes(cache_mla_krope, 2, 3)
    tokens = lambda a: a.reshape(db, 1, a.shape[-1])
    s_s, s_k, s_v, s_c, s_kr = [], [], [], [], []
    for layer in range(depth):
        i = layer // 2
        ng = rowof(norm_gain, layer)
        if layer % 2 == 0:
            qa, lf, ka, ia, ga, qb, kb, vb, _, _, gb = _even_in(xs, ng, lb_logits, w_in_e[i], layer, sb_scale, db)
            oa, s_new = _hgrn_decode(tokens(qa), tokens(lf), tokens(ka), tokens(ia), tokens(ga),
                                     rowof(hgrn_out_gain, i), state_hgrn, i)
            q8 = jnp.pad(qb.reshape(db, b_heads, b_hd), ((0, 0), (0, 8 - b_heads), (0, 0)))
            ob = _sb_decode(q8, sb_k, sb_v, page_table, i, b_heads, pg)[:, :b_heads]
            xs = _even_out(xs, oa.reshape(1, db, -1), ob.reshape(1, db, -1), gb, w_out_e[i], db)
            s_s.append(s_new)
            s_k.append(kb.reshape(db, 1, b_heads, b_hd))
            s_v.append(vb.reshape(db, 1, b_heads, b_hd))
        else:
            qn, qr, kn, krp, c, kr, g = _odd_in(
                xs, ng, w_in_o[i], rowof(q_a_gain, i), w_uq_p[i], rowof(kv_a_gain, i), w_uk_b[i],
                rowof(qk_gain_q_nope, i), rowof(g_qr, i), rowof(qk_gain_k_nope, i), rowof(g_kr, i),
                cos_s, sin_s, dims, db)
            ctx = _mla_decode(tokens(qn), tokens(qr), tokens(kn), tokens(krp), tokens(c),
                              rowof(qk_gain_k_nope, i), w_uk_b[i], w_uk_t[i], cache_mla_ckv, mla_krt,
                              page_table, i, c_heads, pg)
            ctx = ctx[:, :c_heads].reshape(db, c_heads * kvl)
            xs = _odd_out_dec(xs[0], ctx, g[0], w_uv_b[i], w_out_o[i], c_heads)[None]
            s_c.append(c.reshape(db, 1, kvl))
            s_kr.append(kr.reshape(db, 1, c_rope))
    y_sample = xs.reshape(db, 1, d)

    return (y_prompt, y_sample, jnp.stack(p_s), jnp.stack(p_k), jnp.stack(p_v), jnp.stack(p_c), jnp.stack(p_kr),
            jnp.stack(s_s), jnp.stack(s_k), jnp.stack(s_v), jnp.stack(s_c), jnp.stack(s_kr))
```

```python
import functools

import numpy as np
import jax
import jax.numpy as jnp
from jax import lax
from jax.experimental import pallas as pl
from jax.experimental.pallas import tpu as pltpu

EPS = 1e-6
ROPE_THETA = 10000.0
LANES = 128
VMEM_LIMIT = 56 * 1024 * 1024
F32 = jnp.float32
BF16 = jnp.bfloat16
GLA_LEVELS = (64, 32, 16, 8, 4, 2, 1)

NT = (((1,), (1,)), ((), ()))
TN = (((0,), (0,)), ((), ()))


def _params(*sem):
    return pltpu.CompilerParams(dimension_semantics=sem, vmem_limit_bytes=VMEM_LIMIT)


def _dot(a, b):
    return jnp.dot(a, b, preferred_element_type=F32)


def _dot_nt(a, b):
    return lax.dot_general(a, b, NT, preferred_element_type=F32)


def _rms(x, gain, n=None):
    ss = jnp.sum(x * x, axis=-1, keepdims=True)
    ms = ss / (x.shape[-1] if n is None else n)
    return x * lax.rsqrt(ms + EPS) * gain


def _sigmoid(x):
    return 1.0 / (1.0 + jnp.exp(-x))


def _silu(x):
    return x * _sigmoid(x)


def _softplus(z):
    return jnp.maximum(z, 0.0) + jnp.log(1.0 + jnp.exp(-jnp.abs(z)))


def _split2_lanes(x):
    return jnp.concatenate(_split2(x), axis=1)


def _split2(x):
    hi = x.astype(BF16)
    lo = (x - hi.astype(F32)).astype(BF16)
    return hi, lo


def _split3(x):
    hi = x.astype(BF16)
    r = x - hi.astype(F32)
    mid = r.astype(BF16)
    lo = (r - mid.astype(F32)).astype(BF16)
    return hi, mid, lo


def _rope(x, cos_t, sin_t, half):
    lane = lax.broadcasted_iota(jnp.int32, x.shape, 1)
    swapped = jnp.where(lane < half, pltpu.roll(x, LANES - half, 1), pltpu.roll(x, half, 1))
    return x * cos_t + swapped * sin_t


def _even_in_kernel(layer, qb_scale, x_ref, ng_ref, lbl_ref, w_ref,
                    qa_ref, lf_ref, ka_ref, ia_ref, ga_ref, qb_ref, kb_ref, vb_ref, kb16_ref, vb16_ref, gb_ref):
    gw = qa_ref.shape[-1]
    hb = _rms(x_ref[...], ng_ref[...]).astype(BF16)

    def proj(g):
        return _dot(hb, w_ref[:, g * gw:(g + 1) * gw])

    logits = lbl_ref[...]
    e = jnp.exp(logits - jnp.max(logits, axis=0, keepdims=True))
    sm = e / jnp.sum(e, axis=0, keepdims=True)
    cum = sm[0:1]
    for i in range(1, layer + 1):
        cum = cum + sm[i:i + 1]
    lb = cum - sm[0:1]

    qa_ref[...] = proj(0)
    f = lb + (1.0 - lb) * _sigmoid(proj(1))
    lf_ref[...] = jnp.log(f)
    ka_ref[...] = 1.0 - f
    ia_ref[...] = proj(2)
    ga_ref[...] = proj(3)
    qb_ref[...] = (proj(4) * qb_scale).astype(BF16)
    heads = kb_ref.shape[0] // x_ref.shape[0]
    hd = gw // heads
    for src, rows_ref, b16_ref in ((proj(5), kb_ref, kb16_ref), (proj(6), vb_ref, vb16_ref)):
        b16_ref[...] = src.astype(BF16)
        for h in range(heads):
            rows_ref[pl.ds(h, x_ref.shape[0], stride=heads), :] = src[:, h * hd:(h + 1) * hd]
    gb_ref[...] = proj(7)


def _even_in(x, ng, lbl, w, layer, qb_scale, heads, tm):
    bn, ln, d = x.shape
    gw = w.shape[1] // 8
    row = lambda b, j: (b, j, 0)
    const = lambda b, j: (0, 0)
    out_dt = [F32, F32, F32, F32, F32, BF16, F32, F32, BF16, BF16, F32]
    wide = (pl.BlockSpec((None, tm, gw), row), lambda dt: jax.ShapeDtypeStruct((bn, ln, gw), dt))
    rows = (pl.BlockSpec((None, tm * heads, gw // heads), row),
            lambda dt: jax.ShapeDtypeStruct((bn, ln * heads, gw // heads), dt))
    kinds = [wide] * 6 + [rows, rows] + [wide] * 3
    return pl.pallas_call(
        functools.partial(_even_in_kernel, layer, qb_scale),
        grid=(bn, ln // tm),
        in_specs=[pl.BlockSpec((None, tm, d), row), pl.BlockSpec((1, d), const),
                  pl.BlockSpec(lbl.shape, const), pl.BlockSpec(w.shape, const)],
        out_specs=[spec for spec, _ in kinds],
        out_shape=[shape(dt) for (_, shape), dt in zip(kinds, out_dt)],
        compiler_params=_params("parallel", "parallel"),
        name="even_in",
    )(x, ng, lbl, w)


def _gla_constants():
    c = LANES
    t = np.arange(c)[:, None]
    s = np.arange(c)[None, :]
    blocks = [(s <= t)]
    sel, mask = [], []
    for hs in GLA_LEVELS:
        grp = t // (2 * hs)
        mid = grp * 2 * hs + hs - 1
        upper = (t % (2 * hs)) >= hs
        w = np.where(upper, (s > mid) & (s <= t), (s > t) & (s <= mid))
        blocks.append(w)
        sel.append(np.broadcast_to(upper, (c, c)))
        mask.append(upper & ((s % (2 * hs)) < hs) & ((s // (2 * hs)) == grp))
    wall = np.concatenate(blocks, axis=0).astype(np.float32)
    wall = np.concatenate([wall, wall, wall], axis=1)
    return (jnp.asarray(wall, BF16), jnp.asarray(np.stack(sel).astype(np.float32)),
            jnp.asarray(np.stack(mask).astype(np.float32)))


def _hgrn_kernel(n_valid, wall_ref, sel_ref, mask_ref, q_ref, lf_ref, k_ref, v_ref, g_ref, gain_ref,
                 o_ref, s_ref, st_scr):
    c = pl.program_id(1)
    heads = st_scr.shape[0]
    tc = q_ref.shape[0]

    @pl.when(c == 0)
    def _():
        st_scr[...] = jnp.zeros_like(st_scr)

    row = c * tc + lax.broadcasted_iota(jnp.int32, (tc, 1), 0)
    valid = row < n_valid
    lf_all = jnp.where(valid, lf_ref[...], 0.0)
    sums = _dot(wall_ref[...], jnp.concatenate(_split3(lf_all), axis=0))

    hs = [slice(h * LANES, (h + 1) * LANES) for h in range(heads)]
    qs = [q_ref[:, sl] for sl in hs]
    ks = [jnp.where(valid, k_ref[:, sl], 0.0) for sl in hs]
    vbs = [v_ref[:, sl].astype(BF16) for sl in hs]
    bs = [sums[0:tc, sl] for sl in hs]
    xs = [[(jnp.where(sel_ref[li] > 0.5, qs[h], ks[h]) * jnp.exp(sums[(li + 1) * tc:(li + 2) * tc, hs[h]])).astype(BF16)
           for li in range(len(GLA_LEVELS))] for h in range(heads)]
    grams = [[_dot_nt(x, x) for x in xh] for xh in xs]
    atts = []
    for gh in grams:
        att = gh[0] * mask_ref[0]
        for li in range(1, len(GLA_LEVELS)):
            att = att + gh[li] * mask_ref[li]
        atts.append(att.astype(BF16))
    sts = [st_scr[h] for h in range(heads)]
    inter = [_dot_nt((qs[h] * jnp.exp(bs[h])).astype(BF16), sts[h].astype(BF16)) for h in range(heads)]
    intra = [_dot(atts[h], vbs[h]) for h in range(heads)]
    for h, sl in enumerate(hs):
        b_last = bs[h][tc - 1:tc, :]
        kt = (ks[h] * jnp.exp(b_last - bs[h])).astype(BF16)
        st_scr[h] = sts[h] * jnp.exp(b_last) + lax.dot_general(vbs[h], kt, TN, preferred_element_type=F32)
        o = inter[h] + intra[h] + jnp.sum(qs[h] * ks[h], axis=-1, keepdims=True) * v_ref[:, sl]
        o_ref[:, sl] = _rms(o, gain_ref[...]) * _silu(g_ref[:, sl])

    @pl.when(c == pl.num_programs(1) - 1)
    def _():
        for h in range(heads):
            s_ref[h] = st_scr[h].T


def _hgrn_prompt(q, lf, k, v, g, gain, n_valid, heads):
    bn, lp, w = q.shape
    wall, sel, mask = _gla_constants()
    tok = pl.BlockSpec((None, LANES, w), lambda b, c: (b, c, 0))
    c2 = lambda b, c: (0, 0)
    c3 = lambda b, c: (0, 0, 0)
    dk = w // heads
    return pl.pallas_call(
        functools.partial(_hgrn_kernel, n_valid),
        grid=(bn, lp // LANES),
        in_specs=[pl.BlockSpec(wall.shape, c2), pl.BlockSpec(sel.shape, c3), pl.BlockSpec(mask.shape, c3),
                  tok, tok, tok, tok, tok, pl.BlockSpec((1, dk), c2)],
        out_specs=[tok, pl.BlockSpec((None, heads, dk, dk), lambda b, c: (b, 0, 0, 0))],
        out_shape=[jax.ShapeDtypeStruct((bn, lp, w), F32), jax.ShapeDtypeStruct((bn, heads, dk, dk), F32)],
        scratch_shapes=[pltpu.VMEM((heads, dk, dk), F32)],
        compiler_params=_params("parallel", "arbitrary"),
        name="hgrn_prompt",
    )(wall, sel, mask, q, lf, k, v, g, gain)


def _suffix_constant():
    s = np.arange(LANES)
    tri = (s[:, None] > s[None, :]).astype(np.float32)
    uo = np.concatenate([tri, np.ones((LANES, LANES), np.float32)], axis=1)
    return jnp.asarray(np.concatenate([uo, uo], axis=0), BF16)


def _sb_tile(heads, q_ref, k_ref, v_ref, uo, later_scr, acc_scr, off, nsub, first):
    tq = q_ref.shape[0]
    width = nsub * tq
    hd = q_ref.shape[1] // heads
    hs = [slice(h * hd, (h + 1) * hd) for h in range(heads)]
    subs = [slice(j * tq, (j + 1) * tq) for j in range(nsub)]
    zs = [_dot_nt(q_ref[:, sl], k_ref[pl.ds(off, width), sl]) for sl in hs]
    sps = [_softplus(z) for z in zs]
    if first:
        t = lax.broadcasted_iota(jnp.int32, (tq, width), 0)
        s = lax.broadcasted_iota(jnp.int32, (tq, width), 1)
        vis = s < t
        sps = [jnp.where(vis, sp, 0.0) for sp in sps]
    res = _dot(jnp.concatenate([_split2_lanes(sp[:, sub]) for sp in sps for sub in subs], axis=0), uo)
    for h, sl in enumerate(hs):
        later = None if first else later_scr[h]
        parts = [None] * nsub
        for j in range(nsub - 1, -1, -1):
            r = res[(h * nsub + j) * tq:(h * nsub + j + 1) * tq]
            log_a = zs[h][:, subs[j]] - sps[h][:, subs[j]] - r[:, :tq]
            parts[j] = jnp.exp(log_a if later is None else log_a - later)
            later = r[:, tq:] if later is None else later + r[:, tq:]
        later_scr[h] = later
        a = parts[0] if nsub == 1 else jnp.concatenate(parts, axis=1)
        if first:
            a = jnp.where(vis, a, 0.0)
        av = _dot(a.astype(BF16), v_ref[pl.ds(off, width), sl])
        acc_scr[:, sl] = av if first else acc_scr[:, sl] + av


def _sb_kernel(heads, q_ref, k_ref, v_ref, uo_ref, o_ref, later_scr, acc_scr):
    i = pl.program_id(1)
    tq = q_ref.shape[0]
    uo = uo_ref[...]
    tile = functools.partial(_sb_tile, heads, q_ref, k_ref, v_ref, uo, later_scr, acc_scr)
    tile(pl.multiple_of(i * tq, tq), 1, True)
    pairs = i // 2

    @pl.when(i % 2 == 1)
    def _():
        tile(pl.multiple_of((i - 1) * tq, tq), 1, False)

    def body(jj, carry):
        tile(pl.multiple_of((pairs - 1 - jj) * 2 * tq, 2 * tq), 2, False)
        return carry

    lax.fori_loop(0, pairs, body, 0)
    o_ref[...] = acc_scr[...]


def _sb_prompt(q, k, v, heads):
    bn, lp, w = q.shape
    uo = _suffix_constant()
    qspec = pl.BlockSpec((None, LANES, w), lambda b, i: (b, i, 0))
    kspec = pl.BlockSpec((None, lp, w), lambda b, i: (b, 0, 0))
    return pl.pallas_call(
        functools.partial(_sb_kernel, heads),
        grid=(bn, lp // LANES),
        in_specs=[qspec, kspec, kspec, pl.BlockSpec(uo.shape, lambda b, i: (0, 0))],
        out_specs=qspec,
        out_shape=jax.ShapeDtypeStruct((bn, lp, w), F32),
        scratch_shapes=[pltpu.VMEM((heads, LANES, LANES), F32), pltpu.VMEM((LANES, w), F32)],
        compiler_params=_params("parallel", "arbitrary"),
        name="sb_prompt",
    )(q, k, v, uo)


def _even_out_kernel(x_ref, oa_ref, ob_ref, gb_ref, w_ref, y_ref):
    wa = oa_ref.shape[-1]
    ob = ob_ref[...] * _silu(gb_ref[...])
    y_ref[...] = (x_ref[...] + _dot(oa_ref[...].astype(BF16), w_ref[0:wa, :])
                  + _dot(ob.astype(BF16), w_ref[wa:, :]))


def _even_out(x, oa, ob, gb, w, tm):
    bn, ln, d = x.shape
    row = lambda b, j: (b, j, 0)
    wide = pl.BlockSpec((None, tm, d), row)
    half = pl.BlockSpec((None, tm, oa.shape[-1]), row)
    return pl.pallas_call(
        _even_out_kernel,
        grid=(bn, ln // tm),
        in_specs=[wide, half, half, half, pl.BlockSpec(w.shape, lambda b, j: (0, 0))],
        out_specs=wide,
        out_shape=jax.ShapeDtypeStruct(x.shape, F32),
        compiler_params=_params("parallel", "parallel"),
        name="even_out",
    )(x, oa, ob, gb, w)


def _odd_in_kernel(dims, x_ref, ng_ref, w_ref, qag_ref, wuq_ref, kvg_ref, wuk_ref,
                   gqn_ref, gqr_ref, gkn_ref, gkr_ref, cos_ref, sin_ref,
                   qn_ref, qr_ref, kn_ref, krp_ref, c_ref, kr_ref, g_ref):
    ql, kvl, cr, heads = dims
    scale = (LANES + cr) ** -0.5
    hb = _rms(x_ref[...], ng_ref[...]).astype(BF16)
    y = _dot(hb, w_ref[...])
    o1, o2 = ql + kvl, ql + kvl + LANES
    cos_t, sin_t = cos_ref[...], sin_ref[...]
    g_ref[...] = y[:, o2:]
    c = _rms(y[:, ql:o1], kvg_ref[...])
    c_ref[...] = c
    kr = _rope(_rms(y[:, o1:o2], gkr_ref[...], cr), cos_t, sin_t, cr // 2)
    kr_ref[...] = kr[:, :cr]
    krp_ref[...] = kr.astype(BF16)
    cb = c.astype(BF16)
    kn = _dot(cb, wuk_ref[...])
    q = _dot(_rms(y[:, :ql], qag_ref[...]).astype(BF16), wuq_ref[...])
    nw = heads * LANES
    for h in range(heads):
        sl = slice(h * LANES, (h + 1) * LANES)
        kn_ref[:, sl] = _rms(kn[:, sl], gkn_ref[...]).astype(BF16)
        qn_ref[:, sl] = (_rms(q[:, sl], gqn_ref[...]) * scale).astype(BF16)
        qr = _rms(q[:, nw + h * LANES:nw + (h + 1) * LANES], gqr_ref[...], cr)
        qr_ref[:, sl] = (_rope(qr, cos_t, sin_t, cr // 2) * scale).astype(BF16)


def _odd_in(x, ng, w, qag, wuq, kvg, wuk, gqn, gqr, gkn, gkr, cos_t, sin_t, dims, tm):
    bn, ln, d = x.shape
    ql, kvl, cr, heads = dims
    row = lambda b, j: (b, j, 0)
    const = lambda b, j: (0, 0)
    full = lambda a: pl.BlockSpec(a.shape, const)
    nw = heads * LANES
    outs = [(nw, BF16), (nw, BF16), (nw, BF16), (LANES, BF16), (kvl, F32), (cr, F32), (nw, F32)]
    return pl.pallas_call(
        functools.partial(_odd_in_kernel, dims),
        grid=(bn, ln // tm),
        in_specs=[pl.BlockSpec((None, tm, d), row), full(ng), full(w), full(qag), full(wuq), full(kvg),
                  full(wuk), full(gqn), full(gqr), full(gkn), full(gkr),
                  pl.BlockSpec((tm, LANES), lambda b, j: (j, 0)), pl.BlockSpec((tm, LANES), lambda b, j: (j, 0))],
        out_specs=[pl.BlockSpec((None, tm, n), row) for n, _ in outs],
        out_shape=[jax.ShapeDtypeStruct((bn, ln, n), dt) for n, dt in outs],
        compiler_params=_params("parallel", "parallel"),
        name="odd_in",
    )(x, ng, w, qag, wuq, kvg, wuk, gqn, gqr, gkn, gkr, cos_t, sin_t)


def _mla_vt_kernel(c_ref, wuvt_ref, vt_ref):
    lp = c_ref.shape[0]
    step = 2 * LANES
    for off in range(0, lp, step):
        n = min(step, lp - off)
        vt_ref[:, off:off + n] = _dot_nt(wuvt_ref[...], c_ref[off:off + n, :].astype(BF16)).astype(BF16)


def _mla_values_t(c, wuvt):
    bn, lp, kvl = c.shape
    nv = wuvt.shape[0]
    return pl.pallas_call(
        _mla_vt_kernel,
        grid=(bn,),
        in_specs=[pl.BlockSpec((None, lp, kvl), lambda b: (b, 0, 0)), pl.BlockSpec(wuvt.shape, lambda b: (0, 0))],
        out_specs=pl.BlockSpec((None, nv, lp), lambda b: (b, 0, 0)),
        out_shape=jax.ShapeDtypeStruct((bn, nv, lp), BF16),
        compiler_params=_params("parallel"),
        name="mla_values_t",
    )(c, wuvt)


def _mla_tile(heads, qn_ref, qr_ref, kn_ref, kr_ref, vt_ref, m_scr, l_scr, acc_scr, off, width, first):
    tq = qn_ref.shape[0]
    hd = qn_ref.shape[1] // heads
    hs = [slice(h * hd, (h + 1) * hd) for h in range(heads)]
    kr = kr_ref[pl.ds(off, width), :]
    ss = [_dot_nt(jnp.concatenate([kn_ref[pl.ds(off, width), sl], kr], axis=1),
                  jnp.concatenate([qn_ref[:, sl], qr_ref[:, sl]], axis=1)) for sl in hs]
    if first:
        u = lax.broadcasted_iota(jnp.int32, (width, tq), 0)
        t = lax.broadcasted_iota(jnp.int32, (width, tq), 1)
        ss = [jnp.where(u <= t, s, -jnp.inf) for s in ss]
    ps, alphas = [], []
    for h, s in enumerate(ss):
        mx = jnp.max(s, axis=0, keepdims=True)
        if first:
            m_new = mx
        else:
            m_old = m_scr[h:h + 1, :]
            m_new = jnp.maximum(m_old, mx)
            alphas.append(jnp.exp(m_old - m_new))
        p = jnp.exp(s - m_new)
        col = jnp.sum(p, axis=0, keepdims=True)
        l_scr[h:h + 1, :] = col if first else alphas[h] * l_scr[h:h + 1, :] + col
        m_scr[h:h + 1, :] = m_new
        ps.append(p.astype(BF16))
    for h, sl in enumerate(hs):
        pv = _dot(vt_ref[sl, pl.ds(off, width)], ps[h])
        acc_scr[sl, :] = pv if first else alphas[h] * acc_scr[sl, :] + pv


def _mla_kernel(heads, qn_ref, qr_ref, kn_ref, kr_ref, vt_ref, o_ref, m_scr, l_scr, acc_scr):
    i = pl.program_id(1)
    tq = qn_ref.shape[0]
    hd = qn_ref.shape[1] // heads
    tile = functools.partial(_mla_tile, heads, qn_ref, qr_ref, kn_ref, kr_ref, vt_ref, m_scr, l_scr, acc_scr)
    tile(pl.multiple_of(i * tq, tq), tq, True)

    @pl.when(i % 2 == 1)
    def _():
        tile(pl.multiple_of((i - 1) * tq, tq), tq, False)

    def body(jj, carry):
        tile(pl.multiple_of(jj * 2 * tq, 2 * tq), 2 * tq, False)
        return carry

    lax.fori_loop(0, i // 2, body, 0)
    for h in range(heads):
        sl = slice(h * hd, (h + 1) * hd)
        o_ref[:, sl] = (acc_scr[sl, :] / l_scr[h:h + 1, :]).T


def _mla_prompt(qn, qr, kn, krp, vt, heads):
    bn, lp, w = qn.shape
    qspec = pl.BlockSpec((None, LANES, w), lambda b, i: (b, i, 0))
    kspec = pl.BlockSpec((None, lp, w), lambda b, i: (b, 0, 0))
    return pl.pallas_call(
        functools.partial(_mla_kernel, heads),
        grid=(bn, lp // LANES),
        in_specs=[qspec, qspec, kspec, pl.BlockSpec((None, lp, LANES), lambda b, i: (b, 0, 0)),
                  pl.BlockSpec((None, w, lp), lambda b, i: (b, 0, 0))],
        out_specs=qspec,
        out_shape=jax.ShapeDtypeStruct((bn, lp, w), F32),
        scratch_shapes=[pltpu.VMEM((8, LANES), F32), pltpu.VMEM((8, LANES), F32), pltpu.VMEM((w, LANES), F32)],
        compiler_params=_params("parallel", "arbitrary"),
        name="mla_prompt",
    )(qn, qr, kn, krp, vt)


def _odd_out_kernel(x_ref, o_ref, g_ref, w_ref, y_ref):
    og = (o_ref[...] * _silu(g_ref[...])).astype(BF16)
    y_ref[...] = x_ref[...] + _dot(og, w_ref[...])


def _odd_out(x, o, g, w, tm):
    bn, ln, d = x.shape
    row = lambda b, j: (b, j, 0)
    wide = pl.BlockSpec((None, tm, d), row)
    return pl.pallas_call(
        _odd_out_kernel,
        grid=(bn, ln // tm),
        in_specs=[wide, pl.BlockSpec((None, tm, o.shape[-1]), row), pl.BlockSpec((None, tm, g.shape[-1]), row),
                  pl.BlockSpec(w.shape, lambda b, j: (0, 0))],
        out_specs=wide,
        out_shape=jax.ShapeDtypeStruct(x.shape, F32),
        compiler_params=_params("parallel", "parallel"),
        name="odd_out",
    )(x, o, g, w)


def _odd_out_dec_kernel(heads, x_ref, ctx_ref, g_ref, wuv_ref, w_ref, y_ref):
    kvl = ctx_ref.shape[-1] // heads
    vd = wuv_ref.shape[-1] // heads
    y = x_ref[...]
    for h in range(heads):
        o = _dot(ctx_ref[:, h * kvl:(h + 1) * kvl].astype(BF16), wuv_ref[:, h * vd:(h + 1) * vd])
        og = (o * _silu(g_ref[:, h * vd:(h + 1) * vd])).astype(BF16)
        y = y + _dot(og, w_ref[h * vd:(h + 1) * vd, :])
    y_ref[...] = y


def _odd_out_dec(x, ctx, g, wuv, w, heads):
    full = lambda a: pl.BlockSpec(a.shape, lambda: (0,) * a.ndim)
    return pl.pallas_call(
        functools.partial(_odd_out_dec_kernel, heads),
        in_specs=[full(x), full(ctx), full(g), full(wuv), full(w)],
        out_specs=full(x),
        out_shape=jax.ShapeDtypeStruct(x.shape, F32),
        compiler_params=pltpu.CompilerParams(vmem_limit_bytes=VMEM_LIMIT),
        name="odd_out_dec",
    )(x, ctx, g, wuv, w)


def _column(row):
    n = row.shape[1]
    r = lax.broadcasted_iota(jnp.int32, (n, n), 0)
    c = lax.broadcasted_iota(jnp.int32, (n, n), 1)
    return jnp.sum(jnp.where(r == c, jnp.broadcast_to(row, (n, n)), 0.0), axis=1, keepdims=True)


def _hgrn_dec_kernel(q_ref, lf_ref, k_ref, v_ref, g_ref, gain_ref, s0_ref, o_ref, s_ref):
    heads = s0_ref.shape[0]
    for h in range(heads):
        sl = slice(h * LANES, (h + 1) * LANES)
        q, lf, k, v = q_ref[:, sl], lf_ref[:, sl], k_ref[:, sl], v_ref[:, sl]
        s0 = s0_ref[h]
        decay = jnp.exp(lf)
        qe = jnp.broadcast_to(q * decay, (8, LANES)).astype(BF16)
        o = (_dot(qe, s0.astype(BF16))[0:1] + jnp.sum(q * k, axis=-1, keepdims=True) * v)
        s_ref[h] = _column(decay) * s0 + _column(k) * v
        o_ref[:, sl] = _rms(o, gain_ref[...]) * _silu(g_ref[:, sl])


def _hgrn_decode(q, lf, k, v, g, gain, state, e):
    db, _, w = q.shape
    heads, dk, dv = state.shape[2:]
    tok = pl.BlockSpec((None, 1, w), lambda b: (b, 0, 0))
    return pl.pallas_call(
        _hgrn_dec_kernel,
        grid=(db,),
        in_specs=[tok, tok, tok, tok, tok, pl.BlockSpec((1, dv), lambda b: (0, 0)),
                  pl.BlockSpec((None, None, heads, dk, dv), lambda b: (e, b, 0, 0, 0))],
        out_specs=[tok, pl.BlockSpec((None, heads, dk, dv), lambda b: (b, 0, 0, 0))],
        out_shape=[jax.ShapeDtypeStruct((db, 1, w), F32), jax.ShapeDtypeStruct((db, heads, dk, dv), F32)],
        compiler_params=_params("parallel"),
        name="hgrn_decode",
    )(q, lf, k, v, g, gain, state)


def _interleave_constants(heads, ps):
    c = np.arange(heads * ps)
    dei = (c[:, None] // heads == np.arange(ps)[None, :]).astype(np.float32)
    return jnp.asarray(np.concatenate([dei, dei], axis=0), BF16), jnp.asarray(dei.T, BF16)


def _sb_dec_kernel(heads, pg, pt_ref, q_ref, uo_ref, dei_ref, rei_ref, *refs):
    k_refs, v_refs = refs[:pg], refs[pg:2 * pg]
    o_ref, acc_scr, later_scr = refs[2 * pg:]
    g = pl.program_id(1)

    @pl.when(g == 0)
    def _():
        acc_scr[...] = jnp.zeros_like(acc_scr)
        later_scr[...] = jnp.zeros_like(later_scr)

    q8 = q_ref[...]
    n = k_refs[0].shape[0]
    ps = n // heads
    m = pg * 8
    own = (lax.broadcasted_iota(jnp.int32, (m, n), 1) % heads) == (lax.broadcasted_iota(jnp.int32, (m, n), 0) % 8)
    order = list(range(pg - 1, -1, -1))

    zm = jnp.where(own, jnp.concatenate([_dot_nt(q8, k_refs[r][...].astype(BF16)) for r in order], axis=0), 0.0)
    z = _dot(_split2_lanes(zm), dei_ref[...])
    sp = _softplus(z)
    res = _dot(_split2_lanes(sp), uo_ref[...])
    later = later_scr[...]
    laters = []
    for i in range(pg):
        laters.append(later)
        later = later + res[8 * i:8 * i + 8, ps:]
    later_scr[...] = later
    a = jnp.exp(z - sp - res[:, :ps] - jnp.concatenate(laters, axis=0))
    a_rows = jnp.where(own, _dot(a.astype(BF16), rei_ref[...]), 0.0)
    acc = acc_scr[...]
    for i, r in enumerate(order):
        acc = acc + _dot(a_rows[8 * i:8 * i + 8].astype(BF16), v_refs[r][...].astype(BF16))
    acc_scr[...] = acc

    @pl.when(g == pl.num_programs(1) - 1)
    def _():
        o_ref[...] = acc


def _sb_decode(q8, cache_k, cache_v, page_table, e, heads, pg):
    db, _, hd = q8.shape
    n_pages = page_table.shape[1]
    n = cache_k.shape[2]
    ps = n // heads
    ng = n_pages // pg
    uo = _suffix_constant()
    dei, rei = _interleave_constants(heads, ps)

    def page_spec(r):
        return pl.BlockSpec((None, None, n, hd), lambda b, g, pt: (e, pt[b, (ng - 1 - g) * pg + r], 0, 0))

    tok = pl.BlockSpec((None, 8, hd), lambda b, g, pt: (b, 0, 0))
    const = lambda a: pl.BlockSpec(a.shape, lambda b, g, pt: (0, 0))
    grid_spec = pltpu.PrefetchScalarGridSpec(
        num_scalar_prefetch=1,
        grid=(db, ng),
        in_specs=[tok, const(uo), const(dei), const(rei)] + [page_spec(r) for r in range(pg)] * 2,
        out_specs=tok,
        scratch_shapes=[pltpu.VMEM((8, hd), F32), pltpu.VMEM((8, ps), F32)],
    )
    return pl.pallas_call(
        functools.partial(_sb_dec_kernel, heads, pg),
        grid_spec=grid_spec,
        out_shape=jax.ShapeDtypeStruct((db, 8, hd), F32),
        compiler_params=_params("parallel", "arbitrary"),
        name="sb_decode",
    )(page_table, q8, uo, dei, rei, *([cache_k] * pg), *([cache_v] * pg))


def _mla_dec_kernel(heads, pg, cr, pt_ref, qn_ref, qr_ref, knn_ref, krn_ref, cn_ref, gkn_ref,
                    wuk_ref, wukt_ref, *refs):
    c_refs, kr_refs = refs[:pg], refs[pg:2 * pg]
    ctx_ref, lhs_scr, qr_scr, cbf_scr, m_scr, l_scr, acc_scr = refs[2 * pg:]
    g = pl.program_id(1)
    nk = wukt_ref.shape[0]
    nope = nk // heads
    kvl = wukt_ref.shape[1]

    @pl.when(g == 0)
    def _():
        lhs_scr[0:nk, :] = wukt_ref[...]
        qg = qn_ref[...].astype(F32)
        self_s = jnp.zeros((8, LANES), F32)
        rowid = lax.broadcasted_iota(jnp.int32, (8, LANES), 0)
        absorbed = jnp.zeros((8, kvl), F32)
        arow = lax.broadcasted_iota(jnp.int32, (8, kvl), 0)
        for h in range(heads):
            sl = slice(h * nope, (h + 1) * nope)
            qh = qg[:, sl]
            qt = jnp.broadcast_to(qh * gkn_ref[...], (8, nope)).astype(BF16)
            u = _dot_nt(qt, wuk_ref[:, sl])
            absorbed = jnp.where(arow == h, u, absorbed)
            qr_h = qr_ref[:, h * LANES:(h + 1) * LANES].astype(F32)
            qr_scr[h:h + 1, :] = qr_h
            s_h = (jnp.sum(qh * knn_ref[:, sl].astype(F32), axis=-1, keepdims=True)
                   + jnp.sum(qr_h * krn_ref[...].astype(F32), axis=-1, keepdims=True))
            self_s = jnp.where(rowid == h, jnp.broadcast_to(s_h, (8, LANES)), self_s)
        lhs_scr[nk:nk + 16, :] = jnp.concatenate([absorbed, jnp.zeros_like(absorbed)], axis=0).astype(BF16)
        m_scr[...] = self_s
        l_scr[...] = jnp.ones_like(l_scr)
        acc_scr[...] = jnp.broadcast_to(cn_ref[...], acc_scr.shape)

    ps = c_refs[0].shape[0]
    for r in range(pg):
        cbf_scr[r * ps:(r + 1) * ps, :] = c_refs[r][...].astype(BF16)
    qr8 = qr_scr[...][:, :cr].astype(BF16)
    lhs = lhs_scr[...]
    width = 2 * ps
    kts = [_dot_nt(lhs, cbf_scr[t * width:(t + 1) * width, :]) for t in range(pg * ps // width)]
    raw = []
    for kt in kts:
        ssq = jnp.sum((kt[0:nk] * kt[0:nk]).reshape(heads, nope, width), axis=1)
        raw.append(kt[nk:nk + 8] * lax.rsqrt(ssq / nope + EPS))
    krt = jnp.concatenate([kr_refs[r][...] for r in range(pg)], axis=1).astype(BF16)
    s = jnp.concatenate(raw, axis=1) + _dot(qr8, krt)
    m, l = m_scr[...], l_scr[...]
    m_new = jnp.maximum(m, jnp.max(s, axis=-1, keepdims=True))
    alpha = jnp.exp(m - m_new)
    p = jnp.exp(s - m_new[:, 0:1])
    l = alpha * l + jnp.sum(p, axis=-1, keepdims=True)
    acc = alpha[:, 0:1] * acc_scr[...] + _dot(p.astype(BF16), cbf_scr[...])
    m_scr[...], l_scr[...], acc_scr[...] = m_new, l, acc

    @pl.when(g == pl.num_programs(1) - 1)
    def _():
        ctx_ref[...] = acc / l[:, 0:1]


def _mla_decode(qn, qr, knn, krn, cn, gkn, wuk, wukt, cache_c, cache_krt, page_table, o, heads, pg):
    db = qn.shape[0]
    n_pages = page_table.shape[1]
    ps, kvl = cache_c.shape[2:]
    cr = cache_krt.shape[2]
    ng = n_pages // pg
    nk = wukt.shape[0]

    def page_spec(r, rows, cols):
        return pl.BlockSpec((None, None, rows, cols), lambda b, g, pt: (o, pt[b, g * pg + r], 0, 0))

    tok = lambda a: pl.BlockSpec((None, 1, a.shape[-1]), lambda b, g, pt: (b, 0, 0))
    const = lambda a: pl.BlockSpec(a.shape, lambda b, g, pt: (0, 0))
    grid_spec = pltpu.PrefetchScalarGridSpec(
        num_scalar_prefetch=1,
        grid=(db, ng),
        in_specs=[tok(qn), tok(qr), tok(knn), tok(krn), tok(cn), const(gkn), const(wuk), const(wukt)]
                 + [page_spec(r, ps, kvl) for r in range(pg)] + [page_spec(r, cr, ps) for r in range(pg)],
        out_specs=pl.BlockSpec((None, 8, kvl), lambda b, g, pt: (b, 0, 0)),
        scratch_shapes=[pltpu.VMEM((nk + 16, kvl), BF16), pltpu.VMEM((8, LANES), F32),
                        pltpu.VMEM((pg * ps, kvl), BF16), pltpu.VMEM((8, LANES), F32),
                        pltpu.VMEM((8, LANES), F32), pltpu.VMEM((8, kvl), F32)],
    )
    return pl.pallas_call(
        functools.partial(_mla_dec_kernel, heads, pg, cr),
        grid_spec=grid_spec,
        out_shape=jax.ShapeDtypeStruct((db, 8, kvl), F32),
        compiler_params=_params("parallel", "arbitrary"),
        name="mla_decode",
    )(page_table, qn, qr, knn, krn, cn, gkn, wuk, wukt, *([cache_c] * pg), *([cache_krt] * pg))


def _rope_tables(pos, cr):
    half = cr // 2
    inv = ROPE_THETA ** (-jnp.arange(half, dtype=F32) / half)
    ang = pos.astype(F32)[:, None] * inv[None, :]
    cos, sin = jnp.cos(ang), jnp.sin(ang)
    pad = LANES - cr
    n = pos.shape[0]
    return (jnp.concatenate([cos, cos, jnp.ones((n, pad), F32)], axis=1),
            jnp.concatenate([-sin, sin, jnp.zeros((n, pad), F32)], axis=1))


def _pad_lanes(a, width):
    return jnp.pad(a, [(0, 0)] * (a.ndim - 1) + [(0, width - a.shape[-1])])


def kernel(x_prompt, x_sample, state_hgrn, cache_sb_k, cache_sb_v, cache_mla_ckv, cache_mla_krope, page_table,
           meta_tokens, norm_gain, lb_logits, w_in_even, hgrn_out_gain, w_out_even, w_in_odd, q_a_gain, w_uq,
           kv_a_gain, w_uk, w_uv, qk_gain_q_nope, qk_gain_q_rope, qk_gain_k_nope, qk_gain_k_rope, w_out_odd):
    bp, seq, d = x_prompt.shape
    db, dec_t, _ = x_sample.shape
    assert dec_t == 1, "the sample-group kernels handle exactly one new token per sequence"
    n_meta = meta_tokens.shape[0]
    n_tok = seq + n_meta
    lp = -(-n_tok // LANES) * LANES
    depth = norm_gain.shape[0]
    a_heads, a_kd, a_vd = state_hgrn.shape[2:]
    n_phys, ps, b_heads, b_hd = cache_sb_k.shape[1:]
    assert a_kd == LANES and a_vd == LANES and b_hd == LANES and ps == LANES
    n_pages = page_table.shape[1]
    past = n_pages * ps
    ql, kvl = q_a_gain.shape[1], kv_a_gain.shape[1]
    c_nope, c_rope = qk_gain_q_nope.shape[1], qk_gain_q_rope.shape[1]
    c_heads = w_uk.shape[2] // c_nope
    assert c_nope == LANES and w_uv.shape[2] == c_heads * LANES and c_heads == 8 and c_rope <= LANES
    assert ql % LANES == 0 and kvl % LANES == 0
    sb_scale = b_hd ** -0.5
    dims = (ql, kvl, c_rope, c_heads)
    tm = lp // 4 if (lp // 4) % 16 == 0 else LANES
    pg = 16 if n_pages % 16 == 0 else 2
    assert n_pages % pg == 0

    w_in_e = w_in_even.astype(BF16)
    w_out_e = w_out_even.astype(BF16)
    w_in_o = jnp.concatenate([w_in_odd[..., :ql + kvl], _pad_lanes(w_in_odd[..., ql + kvl:ql + kvl + c_rope], LANES),
                              w_in_odd[..., ql + kvl + c_rope:]], axis=-1).astype(BF16)
    wq = w_uq.reshape(w_uq.shape[0], ql, c_heads, c_nope + c_rope)
    w_uq_p = jnp.concatenate([wq[..., :c_nope].reshape(-1, ql, c_heads * c_nope),
                              _pad_lanes(wq[..., c_nope:], LANES).reshape(-1, ql, c_heads * LANES)],
                             axis=-1).astype(BF16)
    w_uk_b = w_uk.astype(BF16)
    w_uk_t = jnp.swapaxes(w_uk, 1, 2).astype(BF16)
    w_uv_b = w_uv.astype(BF16)
    w_uv_t = jnp.swapaxes(w_uv, 1, 2).astype(BF16)
    w_out_o = w_out_odd.astype(BF16)
    g_qr = _pad_lanes(qk_gain_q_rope, LANES)
    g_kr = _pad_lanes(qk_gain_k_rope, LANES)

    def rowof(a, i):
        return a[i:i + 1]

    meta = jnp.broadcast_to(meta_tokens[None].astype(x_prompt.dtype), (bp, n_meta, d))
    x = jnp.concatenate([meta, x_prompt, jnp.zeros((bp, lp - n_tok, d), x_prompt.dtype)], axis=1)
    cos_p, sin_p = _rope_tables(jnp.arange(lp, dtype=jnp.int32), c_rope)
    p_s, p_k, p_v, p_c, p_kr = [], [], [], [], []
    for layer in range(depth):
        i = layer // 2
        ng = rowof(norm_gain, layer)
        if layer % 2 == 0:
            qa, lf, ka, ia, ga, qb, kb, vb, kb16, vb16, gb = _even_in(x, ng, lb_logits, w_in_e[i], layer, sb_scale,
                                                                      b_heads, tm)
            oa, s_new = _hgrn_prompt(qa, lf, ka, ia, ga, rowof(hgrn_out_gain, i), n_tok, a_heads)
            ob = _sb_prompt(qb, kb16, vb16, b_heads)
            x = _even_out(x, oa, ob, gb, w_out_e[i], tm)
            p_s.append(s_new)
            p_k.append(kb[:, :n_tok * b_heads].reshape(bp, n_tok, b_heads, b_hd))
            p_v.append(vb[:, :n_tok * b_heads].reshape(bp, n_tok, b_heads, b_hd))
        else:
            qn, qr, kn, krp, c, kr, g = _odd_in(
                x, ng, w_in_o[i], rowof(q_a_gain, i), w_uq_p[i], rowof(kv_a_gain, i), w_uk_b[i],
                rowof(qk_gain_q_nope, i), rowof(g_qr, i), rowof(qk_gain_k_nope, i), rowof(g_kr, i),
                cos_p, sin_p, dims, tm)
            o = _mla_prompt(qn, qr, kn, krp, _mla_values_t(c, w_uv_t[i]), c_heads)
            x = _odd_out(x, o, g, w_out_o[i], tm)
            p_c.append(c[:, :n_tok])
            p_kr.append(kr[:, :n_tok])
    y_prompt = x[:, n_meta:n_tok]

    xs = x_sample.reshape(1, db, d)
    cos_s, sin_s = _rope_tables(jnp.full((db,), past, jnp.int32), c_rope)
    sb_k = cache_sb_k.reshape(cache_sb_k.shape[0], n_phys, ps * b_heads, b_hd)
    sb_v = cache_sb_v.reshape(cache_sb_v.shape[0], n_phys, ps * b_heads, b_hd)
    mla_krt = jnp.swapaxes(cache_mla_krope, 2, 3)
    tokens = lambda a: a.reshape(db, 1, a.shape[-1])
    s_s, s_k, s_v, s_c, s_kr = [], [], [], [], []
    for layer in range(depth):
        i = layer // 2
        ng = rowof(norm_gain, layer)
        if layer % 2 == 0:
            qa, lf, ka, ia, ga, qb, kb, vb, _, _, gb = _even_in(xs, ng, lb_logits, w_in_e[i], layer, sb_scale,
                                                                b_heads, db)
            oa, s_new = _hgrn_decode(tokens(qa), tokens(lf), tokens(ka), tokens(ia), tokens(ga),
                                     rowof(hgrn_out_gain, i), state_hgrn, i)
            q8 = jnp.pad(qb.reshape(db, b_heads, b_hd), ((0, 0), (0, 8 - b_heads), (0, 0)))
            ob = _sb_decode(q8, sb_k, sb_v, page_table, i, b_heads, pg)[:, :b_heads]
            xs = _even_out(xs, oa.reshape(1, db, -1), ob.reshape(1, db, -1), gb, w_out_e[i], db)
            s_s.append(s_new)
            s_k.append(kb.reshape(db, 1, b_heads, b_hd))
            s_v.append(vb.reshape(db, 1, b_heads, b_hd))
        else:
            qn, qr, kn, krp, c, kr, g = _odd_in(
                xs, ng, w_in_o[i], rowof(q_a_gain, i), w_uq_p[i], rowof(kv_a_gain, i), w_uk_b[i],
                rowof(qk_gain_q_nope, i), rowof(g_qr, i), rowof(qk_gain_k_nope, i), rowof(g_kr, i),
                cos_s, sin_s, dims, db)
            ctx = _mla_decode(tokens(qn), tokens(qr), tokens(kn), tokens(krp), tokens(c),
                              rowof(qk_gain_k_nope, i), w_uk_b[i], w_uk_t[i], cache_mla_ckv, mla_krt,
                              page_table, i, c_heads, pg)
            ctx = ctx[:, :c_heads].reshape(db, c_heads * kvl)
            xs = _odd_out_dec(xs[0], ctx, g[0], w_uv_b[i], w_out_o[i], c_heads)[None]
            s_c.append(c.reshape(db, 1, kvl))
            s_kr.append(kr.reshape(db, 1, c_rope))
    y_sample = xs.reshape(db, 1, d)

    return (y_prompt, y_sample, jnp.stack(p_s), jnp.stack(p_k), jnp.stack(p_v), jnp.stack(p_c), jnp.stack(p_kr),
            jnp.stack(s_s), jnp.stack(s_k), jnp.stack(s_v), jnp.stack(s_c), jnp.stack(s_kr))
```

```python
import functools

import numpy as np
import jax
import jax.numpy as jnp
from jax import lax
from jax.experimental import pallas as pl
from jax.experimental.pallas import tpu as pltpu

EPS = 1e-6
ROPE_THETA = 10000.0
LANES = 128
VMEM_LIMIT = 56 * 1024 * 1024
F32 = jnp.float32
BF16 = jnp.bfloat16
GLA_LEVELS = (64, 32, 16, 8, 4, 2, 1)

NT = (((1,), (1,)), ((), ()))
TN = (((0,), (0,)), ((), ()))


def _params(*sem):
    return pltpu.CompilerParams(dimension_semantics=sem, vmem_limit_bytes=VMEM_LIMIT)


def _dot(a, b):
    return jnp.dot(a, b, preferred_element_type=F32)


def _dot_nt(a, b):
    return lax.dot_general(a, b, NT, preferred_element_type=F32)


def _rms(x, gain, n=None):
    ss = jnp.sum(x * x, axis=-1, keepdims=True)
    ms = ss / (x.shape[-1] if n is None else n)
    return x * lax.rsqrt(ms + EPS) * gain


def _sigmoid(x):
    return 1.0 / (1.0 + jnp.exp(-x))


def _silu(x):
    return x * _sigmoid(x)


def _softplus(z):
    return jnp.maximum(z, 0.0) + jnp.log(1.0 + jnp.exp(-jnp.abs(z)))


def _split2_lanes(x):
    return jnp.concatenate(_split2(x), axis=1)


def _split2(x):
    hi = x.astype(BF16)
    lo = (x - hi.astype(F32)).astype(BF16)
    return hi, lo


def _split3(x):
    hi = x.astype(BF16)
    r = x - hi.astype(F32)
    mid = r.astype(BF16)
    lo = (r - mid.astype(F32)).astype(BF16)
    return hi, mid, lo


def _rope(x, cos_t, sin_t, half):
    lane = lax.broadcasted_iota(jnp.int32, x.shape, 1)
    swapped = jnp.where(lane < half, pltpu.roll(x, LANES - half, 1), pltpu.roll(x, half, 1))
    return x * cos_t + swapped * sin_t


def _even_in_kernel(layer, qb_scale, x_ref, ng_ref, lbl_ref, w_ref,
                    qa_ref, lf_ref, ka_ref, ia_ref, ga_ref, qb_ref, kb_ref, vb_ref, kb16_ref, vb16_ref, gb_ref):
    gw = qa_ref.shape[-1]
    hb = _rms(x_ref[...], ng_ref[...]).astype(BF16)

    def proj(g):
        return _dot(hb, w_ref[:, g * gw:(g + 1) * gw])

    logits = lbl_ref[...]
    e = jnp.exp(logits - jnp.max(logits, axis=0, keepdims=True))
    sm = e / jnp.sum(e, axis=0, keepdims=True)
    cum = sm[0:1]
    for i in range(1, layer + 1):
        cum = cum + sm[i:i + 1]
    lb = cum - sm[0:1]

    qa_ref[...] = proj(0)
    f = lb + (1.0 - lb) * _sigmoid(proj(1))
    lf_ref[...] = jnp.log(f)
    ka_ref[...] = 1.0 - f
    ia_ref[...] = proj(2)
    ga_ref[...] = proj(3)
    qb_ref[...] = (proj(4) * qb_scale).astype(BF16)
    heads = kb_ref.shape[0] // x_ref.shape[0]
    hd = gw // heads
    for src, rows_ref, b16_ref in ((proj(5), kb_ref, kb16_ref), (proj(6), vb_ref, vb16_ref)):
        b16_ref[...] = src.astype(BF16)
        for h in range(heads):
            rows_ref[pl.ds(h, x_ref.shape[0], stride=heads), :] = src[:, h * hd:(h + 1) * hd]
    gb_ref[...] = proj(7)


def _even_in(x, ng, lbl, w, layer, qb_scale, heads, tm):
    bn, ln, d = x.shape
    gw = w.shape[1] // 8
    row = lambda b, j: (b, j, 0)
    const = lambda b, j: (0, 0)
    out_dt = [F32, F32, F32, F32, F32, BF16, F32, F32, BF16, BF16, F32]
    wide = (pl.BlockSpec((None, tm, gw), row), lambda dt: jax.ShapeDtypeStruct((bn, ln, gw), dt))
    rows = (pl.BlockSpec((None, tm * heads, gw // heads), row),
            lambda dt: jax.ShapeDtypeStruct((bn, ln * heads, gw // heads), dt))
    kinds = [wide] * 6 + [rows, rows] + [wide] * 3
    return pl.pallas_call(
        functools.partial(_even_in_kernel, layer, qb_scale),
        grid=(bn, ln // tm),
        in_specs=[pl.BlockSpec((None, tm, d), row), pl.BlockSpec((1, d), const),
                  pl.BlockSpec(lbl.shape, const), pl.BlockSpec(w.shape, const)],
        out_specs=[spec for spec, _ in kinds],
        out_shape=[shape(dt) for (_, shape), dt in zip(kinds, out_dt)],
        compiler_params=_params("parallel", "parallel"),
        name="even_in",
    )(x, ng, lbl, w)


def _gla_constants():
    c = LANES
    t = np.arange(c)[:, None]
    s = np.arange(c)[None, :]
    blocks = [(s <= t)]
    sel, mask = [], []
    for hs in GLA_LEVELS:
        grp = t // (2 * hs)
        mid = grp * 2 * hs + hs - 1
        upper = (t % (2 * hs)) >= hs
        w = np.where(upper, (s > mid) & (s <= t), (s > t) & (s <= mid))
        blocks.append(w)
        sel.append(np.broadcast_to(upper, (c, c)))
        mask.append(upper & ((s % (2 * hs)) < hs) & ((s // (2 * hs)) == grp))
    wall = np.concatenate(blocks, axis=0).astype(np.float32)
    wall = np.concatenate([wall, wall, wall], axis=1)
    return (jnp.asarray(wall, BF16), jnp.asarray(np.stack(sel).astype(np.float32)),
            jnp.asarray(np.stack(mask).astype(np.float32)))


def _hgrn_kernel(n_valid, wall_ref, sel_ref, mask_ref, q_ref, lf_ref, k_ref, v_ref, g_ref, gain_ref,
                 o_ref, s_ref, st_scr):
    c = pl.program_id(1)
    heads = st_scr.shape[0]
    tc = q_ref.shape[0]

    @pl.when(c == 0)
    def _():
        st_scr[...] = jnp.zeros_like(st_scr)

    row = c * tc + lax.broadcasted_iota(jnp.int32, (tc, 1), 0)
    valid = row < n_valid
    lf_all = jnp.where(valid, lf_ref[...], 0.0)
    sums = _dot(wall_ref[...], jnp.concatenate(_split3(lf_all), axis=0))

    hs = [slice(h * LANES, (h + 1) * LANES) for h in range(heads)]
    qs = [q_ref[:, sl] for sl in hs]
    ks = [jnp.where(valid, k_ref[:, sl], 0.0) for sl in hs]
    vbs = [v_ref[:, sl].astype(BF16) for sl in hs]
    bs = [sums[0:tc, sl] for sl in hs]
    xs = [[(jnp.where(sel_ref[li] > 0.5, qs[h], ks[h]) * jnp.exp(sums[(li + 1) * tc:(li + 2) * tc, hs[h]])).astype(BF16)
           for li in range(len(GLA_LEVELS))] for h in range(heads)]
    grams = [[_dot_nt(x, x) for x in xh] for xh in xs]
    atts = []
    for gh in grams:
        att = gh[0] * mask_ref[0]
        for li in range(1, len(GLA_LEVELS)):
            att = att + gh[li] * mask_ref[li]
        atts.append(att.astype(BF16))
    sts = [st_scr[h] for h in range(heads)]
    inter = [_dot_nt((qs[h] * jnp.exp(bs[h])).astype(BF16), sts[h].astype(BF16)) for h in range(heads)]
    intra = [_dot(atts[h], vbs[h]) for h in range(heads)]
    for h, sl in enumerate(hs):
        b_last = bs[h][tc - 1:tc, :]
        kt = (ks[h] * jnp.exp(b_last - bs[h])).astype(BF16)
        st_scr[h] = sts[h] * jnp.exp(b_last) + lax.dot_general(vbs[h], kt, TN, preferred_element_type=F32)
        o = inter[h] + intra[h] + jnp.sum(qs[h] * ks[h], axis=-1, keepdims=True) * v_ref[:, sl]
        o_ref[:, sl] = _rms(o, gain_ref[...]) * _silu(g_ref[:, sl])

    @pl.when(c == pl.num_programs(1) - 1)
    def _():
        for h in range(heads):
            s_ref[h] = st_scr[h].T


def _hgrn_prompt(q, lf, k, v, g, gain, n_valid, heads):
    bn, lp, w = q.shape
    wall, sel, mask = _gla_constants()
    tok = pl.BlockSpec((None, LANES, w), lambda b, c: (b, c, 0))
    c2 = lambda b, c: (0, 0)
    c3 = lambda b, c: (0, 0, 0)
    dk = w // heads
    return pl.pallas_call(
        functools.partial(_hgrn_kernel, n_valid),
        grid=(bn, lp // LANES),
        in_specs=[pl.BlockSpec(wall.shape, c2), pl.BlockSpec(sel.shape, c3), pl.BlockSpec(mask.shape, c3),
                  tok, tok, tok, tok, tok, pl.BlockSpec((1, dk), c2)],
        out_specs=[tok, pl.BlockSpec((None, heads, dk, dk), lambda b, c: (b, 0, 0, 0))],
        out_shape=[jax.ShapeDtypeStruct((bn, lp, w), F32), jax.ShapeDtypeStruct((bn, heads, dk, dk), F32)],
        scratch_shapes=[pltpu.VMEM((heads, dk, dk), F32)],
        compiler_params=_params("parallel", "arbitrary"),
        name="hgrn_prompt",
    )(wall, sel, mask, q, lf, k, v, g, gain)


def _suffix_constant():
    s = np.arange(LANES)
    tri = (s[:, None] > s[None, :]).astype(np.float32)
    uo = np.concatenate([tri, np.ones((LANES, LANES), np.float32)], axis=1)
    return jnp.asarray(np.concatenate([uo, uo], axis=0), BF16)


def _sb_tile(heads, q_ref, k_ref, v_ref, uo, later_scr, acc_scr, off, nsub, first):
    tq = q_ref.shape[0]
    width = nsub * tq
    hd = q_ref.shape[1] // heads
    hs = [slice(h * hd, (h + 1) * hd) for h in range(heads)]
    subs = [slice(j * tq, (j + 1) * tq) for j in range(nsub)]
    zs = [_dot_nt(q_ref[:, sl], k_ref[pl.ds(off, width), sl]) for sl in hs]
    sps = [_softplus(z) for z in zs]
    if first:
        t = lax.broadcasted_iota(jnp.int32, (tq, width), 0)
        s = lax.broadcasted_iota(jnp.int32, (tq, width), 1)
        vis = s < t
        sps = [jnp.where(vis, sp, 0.0) for sp in sps]
    res = _dot(jnp.concatenate([_split2_lanes(sp[:, sub]) for sp in sps for sub in subs], axis=0), uo)
    for h, sl in enumerate(hs):
        later = None if first else later_scr[h]
        parts = [None] * nsub
        for j in range(nsub - 1, -1, -1):
            r = res[(h * nsub + j) * tq:(h * nsub + j + 1) * tq]
            log_a = zs[h][:, subs[j]] - sps[h][:, subs[j]] - r[:, :tq]
            parts[j] = jnp.exp(log_a if later is None else log_a - later)
            later = r[:, tq:] if later is None else later + r[:, tq:]
        later_scr[h] = later
        a = parts[0] if nsub == 1 else jnp.concatenate(parts, axis=1)
        if first:
            a = jnp.where(vis, a, 0.0)
        av = _dot(a.astype(BF16), v_ref[pl.ds(off, width), sl])
        acc_scr[:, sl] = av if first else acc_scr[:, sl] + av


def _sb_kernel(heads, q_ref, k_ref, v_ref, uo_ref, o_ref, later_scr, acc_scr):
    i = pl.program_id(1)
    tq = q_ref.shape[0]
    uo = uo_ref[...]
    tile = functools.partial(_sb_tile, heads, q_ref, k_ref, v_ref, uo, later_scr, acc_scr)
    tile(pl.multiple_of(i * tq, tq), 1, True)
    pairs, quads = i // 2, i // 4

    @pl.when(i % 2 == 1)
    def _():
        tile(pl.multiple_of((i - 1) * tq, tq), 1, False)

    @pl.when(pairs % 2 == 1)
    def _():
        tile(pl.multiple_of((pairs - 1) * 2 * tq, 2 * tq), 2, False)

    def body(jj, carry):
        tile(pl.multiple_of((quads - 1 - jj) * 4 * tq, 4 * tq), 4, False)
        return carry

    lax.fori_loop(0, quads, body, 0)
    o_ref[...] = acc_scr[...]


def _sb_prompt(q, k, v, heads):
    bn, lp, w = q.shape
    uo = _suffix_constant()
    qspec = pl.BlockSpec((None, LANES, w), lambda b, i: (b, i, 0))
    kspec = pl.BlockSpec((None, lp, w), lambda b, i: (b, 0, 0))
    return pl.pallas_call(
        functools.partial(_sb_kernel, heads),
        grid=(bn, lp // LANES),
        in_specs=[qspec, kspec, kspec, pl.BlockSpec(uo.shape, lambda b, i: (0, 0))],
        out_specs=qspec,
        out_shape=jax.ShapeDtypeStruct((bn, lp, w), F32),
        scratch_shapes=[pltpu.VMEM((heads, LANES, LANES), F32), pltpu.VMEM((LANES, w), F32)],
        compiler_params=_params("parallel", "arbitrary"),
        name="sb_prompt",
    )(q, k, v, uo)


def _even_out_kernel(x_ref, oa_ref, ob_ref, gb_ref, w_ref, y_ref):
    wa = oa_ref.shape[-1]
    ob = ob_ref[...] * _silu(gb_ref[...])
    y_ref[...] = (x_ref[...] + _dot(oa_ref[...].astype(BF16), w_ref[0:wa, :])
                  + _dot(ob.astype(BF16), w_ref[wa:, :]))


def _even_out(x, oa, ob, gb, w, tm):
    bn, ln, d = x.shape
    row = lambda b, j: (b, j, 0)
    wide = pl.BlockSpec((None, tm, d), row)
    half = pl.BlockSpec((None, tm, oa.shape[-1]), row)
    return pl.pallas_call(
        _even_out_kernel,
        grid=(bn, ln // tm),
        in_specs=[wide, half, half, half, pl.BlockSpec(w.shape, lambda b, j: (0, 0))],
        out_specs=wide,
        out_shape=jax.ShapeDtypeStruct(x.shape, F32),
        compiler_params=_params("parallel", "parallel"),
        name="even_out",
    )(x, oa, ob, gb, w)


def _odd_in_kernel(dims, x_ref, ng_ref, w_ref, qag_ref, wuq_ref, kvg_ref, wuk_ref,
                   gqn_ref, gqr_ref, gkn_ref, gkr_ref, cos_ref, sin_ref,
                   qn_ref, qr_ref, kn_ref, krp_ref, c_ref, kr_ref, g_ref):
    ql, kvl, cr, heads = dims
    scale = (LANES + cr) ** -0.5
    hb = _rms(x_ref[...], ng_ref[...]).astype(BF16)
    y = _dot(hb, w_ref[...])
    o1, o2 = ql + kvl, ql + kvl + LANES
    cos_t, sin_t = cos_ref[...], sin_ref[...]
    g_ref[...] = y[:, o2:]
    c = _rms(y[:, ql:o1], kvg_ref[...])
    c_ref[...] = c
    kr = _rope(_rms(y[:, o1:o2], gkr_ref[...], cr), cos_t, sin_t, cr // 2)
    kr_ref[...] = kr[:, :cr]
    krp_ref[...] = kr.astype(BF16)
    cb = c.astype(BF16)
    kn = _dot(cb, wuk_ref[...])
    q = _dot(_rms(y[:, :ql], qag_ref[...]).astype(BF16), wuq_ref[...])
    nw = heads * LANES
    for h in range(heads):
        sl = slice(h * LANES, (h + 1) * LANES)
        kn_ref[:, sl] = _rms(kn[:, sl], gkn_ref[...]).astype(BF16)
        qn_ref[:, sl] = (_rms(q[:, sl], gqn_ref[...]) * scale).astype(BF16)
        qr = _rms(q[:, nw + h * LANES:nw + (h + 1) * LANES], gqr_ref[...], cr)
        qr_ref[:, sl] = (_rope(qr, cos_t, sin_t, cr // 2) * scale).astype(BF16)


def _odd_in(x, ng, w, qag, wuq, kvg, wuk, gqn, gqr, gkn, gkr, cos_t, sin_t, dims, tm):
    bn, ln, d = x.shape
    ql, kvl, cr, heads = dims
    row = lambda b, j: (b, j, 0)
    const = lambda b, j: (0, 0)
    full = lambda a: pl.BlockSpec(a.shape, const)
    nw = heads * LANES
    outs = [(nw, BF16), (nw, BF16), (nw, BF16), (LANES, BF16), (kvl, F32), (cr, F32), (nw, F32)]
    return pl.pallas_call(
        functools.partial(_odd_in_kernel, dims),
        grid=(bn, ln // tm),
        in_specs=[pl.BlockSpec((None, tm, d), row), full(ng), full(w), full(qag), full(wuq), full(kvg),
                  full(wuk), full(gqn), full(gqr), full(gkn), full(gkr),
                  pl.BlockSpec((tm, LANES), lambda b, j: (j, 0)), pl.BlockSpec((tm, LANES), lambda b, j: (j, 0))],
        out_specs=[pl.BlockSpec((None, tm, n), row) for n, _ in outs],
        out_shape=[jax.ShapeDtypeStruct((bn, ln, n), dt) for n, dt in outs],
        compiler_params=_params("parallel", "parallel"),
        name="odd_in",
    )(x, ng, w, qag, wuq, kvg, wuk, gqn, gqr, gkn, gkr, cos_t, sin_t)


def _mla_vt_kernel(c_ref, wuvt_ref, vt_ref):
    lp = c_ref.shape[0]
    step = 2 * LANES
    for off in range(0, lp, step):
        n = min(step, lp - off)
        vt_ref[:, off:off + n] = _dot_nt(wuvt_ref[...], c_ref[off:off + n, :].astype(BF16)).astype(BF16)


def _mla_values_t(c, wuvt):
    bn, lp, kvl = c.shape
    nv = wuvt.shape[0]
    return pl.pallas_call(
        _mla_vt_kernel,
        grid=(bn,),
        in_specs=[pl.BlockSpec((None, lp, kvl), lambda b: (b, 0, 0)), pl.BlockSpec(wuvt.shape, lambda b: (0, 0))],
        out_specs=pl.BlockSpec((None, nv, lp), lambda b: (b, 0, 0)),
        out_shape=jax.ShapeDtypeStruct((bn, nv, lp), BF16),
        compiler_params=_params("parallel"),
        name="mla_values_t",
    )(c, wuvt)


def _mla_tile(heads, qn_ref, qr_ref, kn_ref, kr_ref, vt_ref, m_scr, l_scr, acc_scr, off, width, first):
    tq = qn_ref.shape[0]
    hd = qn_ref.shape[1] // heads
    hs = [slice(h * hd, (h + 1) * hd) for h in range(heads)]
    kr = kr_ref[pl.ds(off, width), :]
    ss = [_dot_nt(jnp.concatenate([kn_ref[pl.ds(off, width), sl], kr], axis=1),
                  jnp.concatenate([qn_ref[:, sl], qr_ref[:, sl]], axis=1)) for sl in hs]
    if first:
        u = lax.broadcasted_iota(jnp.int32, (width, tq), 0)
        t = lax.broadcasted_iota(jnp.int32, (width, tq), 1)
        ss = [jnp.where(u <= t, s, -jnp.inf) for s in ss]
    ps, alphas = [], []
    for h, s in enumerate(ss):
        mx = jnp.max(s, axis=0, keepdims=True)
        if first:
            m_new = mx
        else:
            m_old = m_scr[h:h + 1, :]
            m_new = jnp.maximum(m_old, mx)
            alphas.append(jnp.exp(m_old - m_new))
        p = jnp.exp(s - m_new)
        col = jnp.sum(p, axis=0, keepdims=True)
        l_scr[h:h + 1, :] = col if first else alphas[h] * l_scr[h:h + 1, :] + col
        m_scr[h:h + 1, :] = m_new
        ps.append(p.astype(BF16))
    for h, sl in enumerate(hs):
        pv = _dot(vt_ref[sl, pl.ds(off, width)], ps[h])
        acc_scr[sl, :] = pv if first else alphas[h] * acc_scr[sl, :] + pv


def _mla_kernel(heads, qn_ref, qr_ref, kn_ref, kr_ref, vt_ref, o_ref, m_scr, l_scr, acc_scr):
    i = pl.program_id(1)
    tq = qn_ref.shape[0]
    hd = qn_ref.shape[1] // heads
    tile = functools.partial(_mla_tile, heads, qn_ref, qr_ref, kn_ref, kr_ref, vt_ref, m_scr, l_scr, acc_scr)
    tile(pl.multiple_of(i * tq, tq), tq, True)

    @pl.when(i % 2 == 1)
    def _():
        tile(pl.multiple_of((i - 1) * tq, tq), tq, False)

    def body(jj, carry):
        tile(pl.multiple_of(jj * 2 * tq, 2 * tq), 2 * tq, False)
        return carry

    lax.fori_loop(0, i // 2, body, 0)
    for h in range(heads):
        sl = slice(h * hd, (h + 1) * hd)
        o_ref[:, sl] = (acc_scr[sl, :] / l_scr[h:h + 1, :]).T


def _mla_prompt(qn, qr, kn, krp, vt, heads):
    bn, lp, w = qn.shape
    qspec = pl.BlockSpec((None, LANES, w), lambda b, i: (b, i, 0))
    kspec = pl.BlockSpec((None, lp, w), lambda b, i: (b, 0, 0))
    return pl.pallas_call(
        functools.partial(_mla_kernel, heads),
        grid=(bn, lp // LANES),
        in_specs=[qspec, qspec, kspec, pl.BlockSpec((None, lp, LANES), lambda b, i: (b, 0, 0)),
                  pl.BlockSpec((None, w, lp), lambda b, i: (b, 0, 0))],
        out_specs=qspec,
        out_shape=jax.ShapeDtypeStruct((bn, lp, w), F32),
        scratch_shapes=[pltpu.VMEM((8, LANES), F32), pltpu.VMEM((8, LANES), F32), pltpu.VMEM((w, LANES), F32)],
        compiler_params=_params("parallel", "arbitrary"),
        name="mla_prompt",
    )(qn, qr, kn, krp, vt)


def _odd_out_kernel(x_ref, o_ref, g_ref, w_ref, y_ref):
    og = (o_ref[...] * _silu(g_ref[...])).astype(BF16)
    y_ref[...] = x_ref[...] + _dot(og, w_ref[...])


def _odd_out(x, o, g, w, tm):
    bn, ln, d = x.shape
    row = lambda b, j: (b, j, 0)
    wide = pl.BlockSpec((None, tm, d), row)
    return pl.pallas_call(
        _odd_out_kernel,
        grid=(bn, ln // tm),
        in_specs=[wide, pl.BlockSpec((None, tm, o.shape[-1]), row), pl.BlockSpec((None, tm, g.shape[-1]), row),
                  pl.BlockSpec(w.shape, lambda b, j: (0, 0))],
        out_specs=wide,
        out_shape=jax.ShapeDtypeStruct(x.shape, F32),
        compiler_params=_params("parallel", "parallel"),
        name="odd_out",
    )(x, o, g, w)


def _odd_out_dec_kernel(heads, x_ref, ctx_ref, g_ref, wuv_ref, w_ref, y_ref):
    kvl = ctx_ref.shape[-1] // heads
    vd = wuv_ref.shape[-1] // heads
    y = x_ref[...]
    for h in range(heads):
        o = _dot(ctx_ref[:, h * kvl:(h + 1) * kvl].astype(BF16), wuv_ref[:, h * vd:(h + 1) * vd])
        og = (o * _silu(g_ref[:, h * vd:(h + 1) * vd])).astype(BF16)
        y = y + _dot(og, w_ref[h * vd:(h + 1) * vd, :])
    y_ref[...] = y


def _odd_out_dec(x, ctx, g, wuv, w, heads):
    full = lambda a: pl.BlockSpec(a.shape, lambda: (0,) * a.ndim)
    return pl.pallas_call(
        functools.partial(_odd_out_dec_kernel, heads),
        in_specs=[full(x), full(ctx), full(g), full(wuv), full(w)],
        out_specs=full(x),
        out_shape=jax.ShapeDtypeStruct(x.shape, F32),
        compiler_params=pltpu.CompilerParams(vmem_limit_bytes=VMEM_LIMIT),
        name="odd_out_dec",
    )(x, ctx, g, wuv, w)


def _column(row):
    n = row.shape[1]
    r = lax.broadcasted_iota(jnp.int32, (n, n), 0)
    c = lax.broadcasted_iota(jnp.int32, (n, n), 1)
    return jnp.sum(jnp.where(r == c, jnp.broadcast_to(row, (n, n)), 0.0), axis=1, keepdims=True)


def _hgrn_dec_kernel(q_ref, lf_ref, k_ref, v_ref, g_ref, gain_ref, s0_ref, o_ref, s_ref):
    heads = s0_ref.shape[0]
    for h in range(heads):
        sl = slice(h * LANES, (h + 1) * LANES)
        q, lf, k, v = q_ref[:, sl], lf_ref[:, sl], k_ref[:, sl], v_ref[:, sl]
        s0 = s0_ref[h]
        decay = jnp.exp(lf)
        qe = jnp.broadcast_to(q * decay, (8, LANES)).astype(BF16)
        o = (_dot(qe, s0.astype(BF16))[0:1] + jnp.sum(q * k, axis=-1, keepdims=True) * v)
        s_ref[h] = _column(decay) * s0 + _column(k) * v
        o_ref[:, sl] = _rms(o, gain_ref[...]) * _silu(g_ref[:, sl])


def _hgrn_decode(q, lf, k, v, g, gain, state, e):
    db, _, w = q.shape
    heads, dk, dv = state.shape[2:]
    tok = pl.BlockSpec((None, 1, w), lambda b: (b, 0, 0))
    return pl.pallas_call(
        _hgrn_dec_kernel,
        grid=(db,),
        in_specs=[tok, tok, tok, tok, tok, pl.BlockSpec((1, dv), lambda b: (0, 0)),
                  pl.BlockSpec((None, None, heads, dk, dv), lambda b: (e, b, 0, 0, 0))],
        out_specs=[tok, pl.BlockSpec((None, heads, dk, dv), lambda b: (b, 0, 0, 0))],
        out_shape=[jax.ShapeDtypeStruct((db, 1, w), F32), jax.ShapeDtypeStruct((db, heads, dk, dv), F32)],
        compiler_params=_params("parallel"),
        name="hgrn_decode",
    )(q, lf, k, v, g, gain, state)


def _interleave_constants(heads, ps):
    c = np.arange(heads * ps)
    dei = (c[:, None] // heads == np.arange(ps)[None, :]).astype(np.float32)
    return jnp.asarray(np.concatenate([dei, dei], axis=0), BF16), jnp.asarray(dei.T, BF16)


def _sb_dec_kernel(heads, pg, pt_ref, q_ref, uo_ref, dei_ref, rei_ref, *refs):
    k_refs, v_refs = refs[:pg], refs[pg:2 * pg]
    o_ref, acc_scr, later_scr = refs[2 * pg:]
    g = pl.program_id(1)

    @pl.when(g == 0)
    def _():
        acc_scr[...] = jnp.zeros_like(acc_scr)
        later_scr[...] = jnp.zeros_like(later_scr)

    q8 = q_ref[...]
    n = k_refs[0].shape[0]
    ps = n // heads
    m = pg * 8
    own = (lax.broadcasted_iota(jnp.int32, (m, n), 1) % heads) == (lax.broadcasted_iota(jnp.int32, (m, n), 0) % 8)
    order = list(range(pg - 1, -1, -1))

    zm = jnp.where(own, jnp.concatenate([_dot_nt(q8, k_refs[r][...].astype(BF16)) for r in order], axis=0), 0.0)
    z = _dot(_split2_lanes(zm), dei_ref[...])
    sp = _softplus(z)
    res = _dot(_split2_lanes(sp), uo_ref[...])
    later = later_scr[...]
    laters = []
    for i in range(pg):
        laters.append(later)
        later = later + res[8 * i:8 * i + 8, ps:]
    later_scr[...] = later
    a = jnp.exp(z - sp - res[:, :ps] - jnp.concatenate(laters, axis=0))
    a_rows = jnp.where(own, _dot(a.astype(BF16), rei_ref[...]), 0.0)
    acc = acc_scr[...]
    for i, r in enumerate(order):
        acc = acc + _dot(a_rows[8 * i:8 * i + 8].astype(BF16), v_refs[r][...].astype(BF16))
    acc_scr[...] = acc

    @pl.when(g == pl.num_programs(1) - 1)
    def _():
        o_ref[...] = acc


def _sb_decode(q8, cache_k, cache_v, page_table, e, heads, pg):
    db, _, hd = q8.shape
    n_pages = page_table.shape[1]
    n = cache_k.shape[2]
    ps = n // heads
    ng = n_pages // pg
    uo = _suffix_constant()
    dei, rei = _interleave_constants(heads, ps)

    def page_spec(r):
        return pl.BlockSpec((None, None, n, hd), lambda b, g, pt: (e, pt[b, (ng - 1 - g) * pg + r], 0, 0))

    tok = pl.BlockSpec((None, 8, hd), lambda b, g, pt: (b, 0, 0))
    const = lambda a: pl.BlockSpec(a.shape, lambda b, g, pt: (0, 0))
    grid_spec = pltpu.PrefetchScalarGridSpec(
        num_scalar_prefetch=1,
        grid=(db, ng),
        in_specs=[tok, const(uo), const(dei), const(rei)] + [page_spec(r) for r in range(pg)] * 2,
        out_specs=tok,
        scratch_shapes=[pltpu.VMEM((8, hd), F32), pltpu.VMEM((8, ps), F32)],
    )
    return pl.pallas_call(
        functools.partial(_sb_dec_kernel, heads, pg),
        grid_spec=grid_spec,
        out_shape=jax.ShapeDtypeStruct((db, 8, hd), F32),
        compiler_params=_params("parallel", "arbitrary"),
        name="sb_decode",
    )(page_table, q8, uo, dei, rei, *([cache_k] * pg), *([cache_v] * pg))


def _mla_dec_kernel(heads, pg, cr, pt_ref, qn_ref, qr_ref, knn_ref, krn_ref, cn_ref, gkn_ref,
                    wuk_ref, wukt_ref, *refs):
    c_refs, kr_refs = refs[:pg], refs[pg:2 * pg]
    ctx_ref, lhs_scr, qr_scr, cbf_scr, m_scr, l_scr, acc_scr = refs[2 * pg:]
    g = pl.program_id(1)
    nk = wukt_ref.shape[0]
    nope = nk // heads
    kvl = wukt_ref.shape[1]

    @pl.when(g == 0)
    def _():
        lhs_scr[0:nk, :] = wukt_ref[...]
        qg = qn_ref[...].astype(F32)
        self_s = jnp.zeros((8, LANES), F32)
        rowid = lax.broadcasted_iota(jnp.int32, (8, LANES), 0)
        absorbed = jnp.zeros((8, kvl), F32)
        arow = lax.broadcasted_iota(jnp.int32, (8, kvl), 0)
        for h in range(heads):
            sl = slice(h * nope, (h + 1) * nope)
            qh = qg[:, sl]
            qt = jnp.broadcast_to(qh * gkn_ref[...], (8, nope)).astype(BF16)
            u = _dot_nt(qt, wuk_ref[:, sl])
            absorbed = jnp.where(arow == h, u, absorbed)
            qr_h = qr_ref[:, h * LANES:(h + 1) * LANES].astype(F32)
            qr_scr[h:h + 1, :] = qr_h
            s_h = (jnp.sum(qh * knn_ref[:, sl].astype(F32), axis=-1, keepdims=True)
                   + jnp.sum(qr_h * krn_ref[...].astype(F32), axis=-1, keepdims=True))
            self_s = jnp.where(rowid == h, jnp.broadcast_to(s_h, (8, LANES)), self_s)
        lhs_scr[nk:nk + 16, :] = jnp.concatenate([absorbed, jnp.zeros_like(absorbed)], axis=0).astype(BF16)
        m_scr[...] = self_s
        l_scr[...] = jnp.ones_like(l_scr)
        acc_scr[...] = jnp.broadcast_to(cn_ref[...], acc_scr.shape)

    ps = c_refs[0].shape[0]
    for r in range(pg):
        cbf_scr[r * ps:(r + 1) * ps, :] = c_refs[r][...].astype(BF16)
    qr8 = qr_scr[...][:, :cr].astype(BF16)
    lhs = lhs_scr[...]
    width = 2 * ps
    kts = [_dot_nt(lhs, cbf_scr[t * width:(t + 1) * width, :]) for t in range(pg * ps // width)]
    raw = []
    for kt in kts:
        ssq = jnp.sum((kt[0:nk] * kt[0:nk]).reshape(heads, nope, width), axis=1)
        raw.append(kt[nk:nk + 8] * lax.rsqrt(ssq / nope + EPS))
    krt = jnp.concatenate([kr_refs[r][...] for r in range(pg)], axis=1).astype(BF16)
    s = jnp.concatenate(raw, axis=1) + _dot(qr8, krt)
    m, l = m_scr[...], l_scr[...]
    m_new = jnp.maximum(m, jnp.max(s, axis=-1, keepdims=True))
    alpha = jnp.exp(m - m_new)
    p = jnp.exp(s - m_new[:, 0:1])
    l = alpha * l + jnp.sum(p, axis=-1, keepdims=True)
    acc = alpha[:, 0:1] * acc_scr[...] + _dot(p.astype(BF16), cbf_scr[...])
    m_scr[...], l_scr[...], acc_scr[...] = m_new, l, acc

    @pl.when(g == pl.num_programs(1) - 1)
    def _():
        ctx_ref[...] = acc / l[:, 0:1]


def _mla_decode(qn, qr, knn, krn, cn, gkn, wuk, wukt, cache_c, cache_krt, page_table, o, heads, pg):
    db = qn.shape[0]
    n_pages = page_table.shape[1]
    ps, kvl = cache_c.shape[2:]
    cr = cache_krt.shape[2]
    ng = n_pages // pg
    nk = wukt.shape[0]

    def page_spec(r, rows, cols):
        return pl.BlockSpec((None, None, rows, cols), lambda b, g, pt: (o, pt[b, g * pg + r], 0, 0))

    tok = lambda a: pl.BlockSpec((None, 1, a.shape[-1]), lambda b, g, pt: (b, 0, 0))
    const = lambda a: pl.BlockSpec(a.shape, lambda b, g, pt: (0, 0))
    grid_spec = pltpu.PrefetchScalarGridSpec(
        num_scalar_prefetch=1,
        grid=(db, ng),
        in_specs=[tok(qn), tok(qr), tok(knn), tok(krn), tok(cn), const(gkn), const(wuk), const(wukt)]
                 + [page_spec(r, ps, kvl) for r in range(pg)] + [page_spec(r, cr, ps) for r in range(pg)],
        out_specs=pl.BlockSpec((None, 8, kvl), lambda b, g, pt: (b, 0, 0)),
        scratch_shapes=[pltpu.VMEM((nk + 16, kvl), BF16), pltpu.VMEM((8, LANES), F32),
                        pltpu.VMEM((pg * ps, kvl), BF16), pltpu.VMEM((8, LANES), F32),
                        pltpu.VMEM((8, LANES), F32), pltpu.VMEM((8, kvl), F32)],
    )
    return pl.pallas_call(
        functools.partial(_mla_dec_kernel, heads, pg, cr),
        grid_spec=grid_spec,
        out_shape=jax.ShapeDtypeStruct((db, 8, kvl), F32),
        compiler_params=_params("parallel", "arbitrary"),
        name="mla_decode",
    )(page_table, qn, qr, knn, krn, cn, gkn, wuk, wukt, *([cache_c] * pg), *([cache_krt] * pg))


def _rope_tables(pos, cr):
    half = cr // 2
    inv = ROPE_THETA ** (-jnp.arange(half, dtype=F32) / half)
    ang = pos.astype(F32)[:, None] * inv[None, :]
    cos, sin = jnp.cos(ang), jnp.sin(ang)
    pad = LANES - cr
    n = pos.shape[0]
    return (jnp.concatenate([cos, cos, jnp.ones((n, pad), F32)], axis=1),
            jnp.concatenate([-sin, sin, jnp.zeros((n, pad), F32)], axis=1))


def _pad_lanes(a, width):
    return jnp.pad(a, [(0, 0)] * (a.ndim - 1) + [(0, width - a.shape[-1])])


def kernel(x_prompt, x_sample, state_hgrn, cache_sb_k, cache_sb_v, cache_mla_ckv, cache_mla_krope, page_table,
           meta_tokens, norm_gain, lb_logits, w_in_even, hgrn_out_gain, w_out_even, w_in_odd, q_a_gain, w_uq,
           kv_a_gain, w_uk, w_uv, qk_gain_q_nope, qk_gain_q_rope, qk_gain_k_nope, qk_gain_k_rope, w_out_odd):
    bp, seq, d = x_prompt.shape
    db, dec_t, _ = x_sample.shape
    assert dec_t == 1, "the sample-group kernels handle exactly one new token per sequence"
    n_meta = meta_tokens.shape[0]
    n_tok = seq + n_meta
    lp = -(-n_tok // LANES) * LANES
    depth = norm_gain.shape[0]
    a_heads, a_kd, a_vd = state_hgrn.shape[2:]
    n_phys, ps, b_heads, b_hd = cache_sb_k.shape[1:]
    assert a_kd == LANES and a_vd == LANES and b_hd == LANES and ps == LANES
    n_pages = page_table.shape[1]
    past = n_pages * ps
    ql, kvl = q_a_gain.shape[1], kv_a_gain.shape[1]
    c_nope, c_rope = qk_gain_q_nope.shape[1], qk_gain_q_rope.shape[1]
    c_heads = w_uk.shape[2] // c_nope
    assert c_nope == LANES and w_uv.shape[2] == c_heads * LANES and c_heads == 8 and c_rope <= LANES
    assert ql % LANES == 0 and kvl % LANES == 0
    sb_scale = b_hd ** -0.5
    dims = (ql, kvl, c_rope, c_heads)
    tm = lp // 4 if (lp // 4) % 16 == 0 else LANES
    pg = 16 if n_pages % 16 == 0 else 2
    assert n_pages % pg == 0

    w_in_e = w_in_even.astype(BF16)
    w_out_e = w_out_even.astype(BF16)
    w_in_o = jnp.concatenate([w_in_odd[..., :ql + kvl], _pad_lanes(w_in_odd[..., ql + kvl:ql + kvl + c_rope], LANES),
                              w_in_odd[..., ql + kvl + c_rope:]], axis=-1).astype(BF16)
    wq = w_uq.reshape(w_uq.shape[0], ql, c_heads, c_nope + c_rope)
    w_uq_p = jnp.concatenate([wq[..., :c_nope].reshape(-1, ql, c_heads * c_nope),
                              _pad_lanes(wq[..., c_nope:], LANES).reshape(-1, ql, c_heads * LANES)],
                             axis=-1).astype(BF16)
    w_uk_b = w_uk.astype(BF16)
    w_uk_t = jnp.swapaxes(w_uk, 1, 2).astype(BF16)
    w_uv_b = w_uv.astype(BF16)
    w_uv_t = jnp.swapaxes(w_uv, 1, 2).astype(BF16)
    w_out_o = w_out_odd.astype(BF16)
    g_qr = _pad_lanes(qk_gain_q_rope, LANES)
    g_kr = _pad_lanes(qk_gain_k_rope, LANES)

    def rowof(a, i):
        return a[i:i + 1]

    meta = jnp.broadcast_to(meta_tokens[None].astype(x_prompt.dtype), (bp, n_meta, d))
    x = jnp.concatenate([meta, x_prompt, jnp.zeros((bp, lp - n_tok, d), x_prompt.dtype)], axis=1)
    cos_p, sin_p = _rope_tables(jnp.arange(lp, dtype=jnp.int32), c_rope)
    p_s, p_k, p_v, p_c, p_kr = [], [], [], [], []
    for layer in range(depth):
        i = layer // 2
        ng = rowof(norm_gain, layer)
        if layer % 2 == 0:
            qa, lf, ka, ia, ga, qb, kb, vb, kb16, vb16, gb = _even_in(x, ng, lb_logits, w_in_e[i], layer, sb_scale,
                                                                      b_heads, tm)
            oa, s_new = _hgrn_prompt(qa, lf, ka, ia, ga, rowof(hgrn_out_gain, i), n_tok, a_heads)
            ob = _sb_prompt(qb, kb16, vb16, b_heads)
            x = _even_out(x, oa, ob, gb, w_out_e[i], tm)
            p_s.append(s_new)
            p_k.append(kb[:, :n_tok * b_heads].reshape(bp, n_tok, b_heads, b_hd))
            p_v.append(vb[:, :n_tok * b_heads].reshape(bp, n_tok, b_heads, b_hd))
        else:
            qn, qr, kn, krp, c, kr, g = _odd_in(
                x, ng, w_in_o[i], rowof(q_a_gain, i), w_uq_p[i], rowof(kv_a_gain, i), w_uk_b[i],
                rowof(qk_gain_q_nope, i), rowof(g_qr, i), rowof(qk_gain_k_nope, i), rowof(g_kr, i),
                cos_p, sin_p, dims, tm)
            o = _mla_prompt(qn, qr, kn, krp, _mla_values_t(c, w_uv_t[i]), c_heads)
            x = _odd_out(x, o, g, w_out_o[i], tm)
            p_c.append(c[:, :n_tok])
            p_kr.append(kr[:, :n_tok])
    y_prompt = x[:, n_meta:n_tok]

    xs = x_sample.reshape(1, db, d)
    cos_s, sin_s = _rope_tables(jnp.full((db,), past, jnp.int32), c_rope)
    sb_k = cache_sb_k.reshape(cache_sb_k.shape[0], n_phys, ps * b_heads, b_hd)
    sb_v = cache_sb_v.reshape(cache_sb_v.shape[0], n_phys, ps * b_heads, b_hd)
    mla_krt = jnp.swapaxes(cache_mla_krope, 2, 3)
    tokens = lambda a: a.reshape(db, 1, a.shape[-1])
    s_s, s_k, s_v, s_c, s_kr = [], [], [], [], []
    for layer in range(depth):
        i = layer // 2
        ng = rowof(norm_gain, layer)
        if layer % 2 == 0:
            qa, lf, ka, ia, ga, qb, kb, vb, _, _, gb = _even_in(xs, ng, lb_logits, w_in_e[i], layer, sb_scale,
                                                                b_heads, db)
            oa, s_new = _hgrn_decode(tokens(qa), tokens(lf), tokens(ka), tokens(ia), tokens(ga),
                                     rowof(hgrn_out_gain, i), state_hgrn, i)
            q8 = jnp.pad(qb.reshape(db, b_heads, b_hd), ((0, 0), (0, 8 - b_heads), (0, 0)))
            ob = _sb_decode(q8, sb_k, sb_v, page_table, i, b_heads, pg)[:, :b_heads]
            xs = _even_out(xs, oa.reshape(1, db, -1), ob.reshape(1, db, -1), gb, w_out_e[i], db)
            s_s.append(s_new)
            s_k.append(kb.reshape(db, 1, b_heads, b_hd))
            s_v.append(vb.reshape(db, 1, b_heads, b_hd))
        else:
            qn, qr, kn, krp, c, kr, g = _odd_in(
                xs, ng, w_in_o[i], rowof(q_a_gain, i), w_uq_p[i], rowof(kv_a_gain, i), w_uk_b[i],
                rowof(qk_gain_q_nope, i), rowof(g_qr, i), rowof(qk_gain_k_nope, i), rowof(g_kr, i),
                cos_s, sin_s, dims, db)
            ctx = _mla_decode(tokens(qn), tokens(qr), tokens(kn), tokens(krp), tokens(c),
                              rowof(qk_gain_k_nope, i), w_uk_b[i], w_uk_t[i], cache_mla_ckv, mla_krt,
                              page_table, i, c_heads, pg)
            ctx = ctx[:, :c_heads].reshape(db, c_heads * kvl)
            xs = _odd_out_dec(xs[0], ctx, g[0], w_uv_b[i], w_out_o[i], c_heads)[None]
            s_c.append(c.reshape(db, 1, kvl))
            s_kr.append(kr.reshape(db, 1, c_rope))
    y_sample = xs.reshape(db, 1, d)

    return (y_prompt, y_sample, jnp.stack(p_s), jnp.stack(p_k), jnp.stack(p_v), jnp.stack(p_c), jnp.stack(p_kr),
            jnp.stack(s_s), jnp.stack(s_k), jnp.stack(s_v), jnp.stack(s_c), jnp.stack(s_kr))
```
